```python
import math
import jax, jax.numpy as jnp
from jax import lax
import numpy as np

D_MODEL = 2048
BATCH = 4
SEQ = 2048
DEPTH = 4
DEC_BATCH = 128
DEC_SEQ = 8
PAST_LEN = 16384
PAGE_SIZE = 128

D_MIX = 2 * D_MODEL
D_SSM = D_MIX // 2
D_CONV = D_MIX - D_SSM
SSD_HEAD_DIM = 64
SSD_HEADS = D_SSM // SSD_HEAD_DIM
SSD_GROUPS = 4
SSD_REP = SSD_HEADS // SSD_GROUPS
SSD_STATE = 128
SSD_CONV_W = 4
SSD_CHUNK = 128
D_XBC = D_SSM + 2 * SSD_GROUPS * SSD_STATE
CONF_CONV_W = 31
D_FF = 4 * D_MODEL
Z_END = D_SSM
XBC_END = Z_END + D_XBC
DT_END = XBC_END + SSD_HEADS
N_IN = DT_END + 2 * D_CONV
EPS = 1e-6

kernel_name = "hymba_ssd_conformer_sandwich_decoder_step"


def rmsnorm(x, w):
    xf = x.astype(jnp.float32)
    r = lax.rsqrt(jnp.mean(xf * xf, axis=-1, keepdims=True) + EPS)
    return (xf * r).astype(x.dtype) * w


def layernorm(x, w, b):
    xf = x.astype(jnp.float32)
    mu = jnp.mean(xf, axis=-1, keepdims=True)
    var = jnp.mean(jnp.square(xf - mu), axis=-1, keepdims=True)
    return ((xf - mu) * lax.rsqrt(var + EPS)).astype(x.dtype) * w + b


def causal_depthwise_conv(u, buf, w, b):
    ext = jnp.concatenate([buf.astype(u.dtype), u], axis=1)
    ch = u.shape[-1]
    out = lax.conv_general_dilated(ext, w[:, None, :].astype(u.dtype), window_strides=(1,), padding='VALID',
                                   dimension_numbers=('NWC', 'WIO', 'NWC'), feature_group_count=ch)
    new_buf = ext[:, -(w.shape[0] - 1):]
    return out + b, new_buf


def ssd_scan(x, dt, A, B, C, h0):
    b, l = x.shape[:2]
    q = min(SSD_CHUNK, l)
    nc = -(-l // q)
    pad = nc * q - l
    if pad:
        padf = lambda a: jnp.pad(a, [(0, 0), (0, pad)] + [(0, 0)] * (a.ndim - 2))
        x, dt, B, C = padf(x), padf(dt), padf(B), padf(C)
    x = x.reshape(b, nc, q, *x.shape[2:])
    dt = dt.reshape(b, nc, q, *dt.shape[2:])
    B = B.reshape(b, nc, q, *B.shape[2:])
    C = C.reshape(b, nc, q, *C.shape[2:])
    cum = jnp.cumsum(dt * A, axis=2)
    causal = jnp.tril(jnp.ones((q, q), dtype=bool))[None, None, :, :, None, None]
    seg = cum[:, :, :, None] - cum[:, :, None, :]
    Lmat = jnp.exp(jnp.where(causal, seg, -jnp.inf))
    CB = jnp.einsum('bcign,bcjgn->bcijg', C, B)
    W = CB[..., None] * Lmat * dt[:, :, None]
    y_diag = jnp.einsum('bcijgr,bcjgrp->bcigrp', W, x)
    decay_to_end = jnp.exp(cum[:, :, -1:] - cum)
    chunk_states = jnp.einsum('bcjgn,bcjgr,bcjgrp->bcgrpn', B, decay_to_end * dt, x)
    chunk_decay = jnp.exp(cum[:, :, -1])

    def step(h, inp):
        d, s = inp
        return d[..., None, None] * h + s, h

    h_final, h_starts = lax.scan(step, h0, (jnp.moveaxis(chunk_decay, 1, 0), jnp.moveaxis(chunk_states, 1, 0)))
    h_starts = jnp.moveaxis(h_starts, 0, 1)
    y_off = jnp.einsum('bcign,bcgrpn->bcigrp', C, h_starts) * jnp.exp(cum)[..., None]
    y = (y_diag + y_off).reshape(b, nc * q, *x.shape[3:])[:, :l]
    return y, h_final


def mixer(u, h0, ssd_buf, conf_buf, w_in, ssd_conv_w, ssd_conv_b, dt_bias, a_log, d_skip, ssd_norm_w,
          conf_conv_w, conf_conv_b, conf_norm_w, conf_norm_b, w_out):
    b, l, _ = u.shape
    dtype = u.dtype
    proj = u @ w_in
    z = proj[..., :Z_END]
    xbc = proj[..., Z_END:XBC_END]
    dt_raw = proj[..., XBC_END:DT_END]
    glu = proj[..., DT_END:]
    xbc_c, new_ssd_buf = causal_depthwise_conv(xbc, ssd_buf, ssd_conv_w, ssd_conv_b)
    xbc_c = jax.nn.silu(xbc_c)
    xs = xbc_c[..., :D_SSM]
    Bm = xbc_c[..., D_SSM:D_SSM + SSD_GROUPS * SSD_STATE]
    Cm = xbc_c[..., D_SSM + SSD_GROUPS * SSD_STATE:]
    dt = jax.nn.softplus(dt_raw.astype(jnp.float32) + dt_bias.astype(jnp.float32))
    A = -jnp.exp(a_log.astype(jnp.float32))
    x5 = xs.astype(jnp.float32).reshape(b, l, SSD_GROUPS, SSD_REP, SSD_HEAD_DIM)
    y, h_new = ssd_scan(x5, dt.reshape(b, l, SSD_GROUPS, SSD_REP), A.reshape(SSD_GROUPS, SSD_REP),
                        Bm.astype(jnp.float32).reshape(b, l, SSD_GROUPS, SSD_STATE),
                        Cm.astype(jnp.float32).reshape(b, l, SSD_GROUPS, SSD_STATE),
                        h0.astype(jnp.float32).reshape(b, SSD_GROUPS, SSD_REP, SSD_HEAD_DIM, SSD_STATE))
    y = y + d_skip.astype(jnp.float32).reshape(SSD_GROUPS, SSD_REP)[..., None] * x5
    g = y.reshape(b, l, D_SSM) * jax.nn.silu(z.astype(jnp.float32))
    g = g.reshape(b, l, SSD_GROUPS, D_SSM // SSD_GROUPS)
    g = g * lax.rsqrt(jnp.mean(g * g, axis=-1, keepdims=True) + EPS)
    ssd_out = g.reshape(b, l, D_SSM).astype(dtype) * ssd_norm_w
    v = glu[..., :D_CONV] * jax.nn.sigmoid(glu[..., D_CONV:])
    vc, new_conf_buf = causal_depthwise_conv(v, conf_buf, conf_conv_w, conf_conv_b)
    conf_out = jax.nn.silu(layernorm(vc, conf_norm_w, conf_norm_b))
    mix = jnp.concatenate([ssd_out, conf_out], axis=-1) @ w_out
    h_new = h_new.reshape(b, SSD_HEADS, SSD_HEAD_DIM, SSD_STATE).astype(dtype)
    return mix, h_new, new_ssd_buf, new_conf_buf


def trunk(x, st_ssm, st_ssd_conv, st_conf, w_in, ssd_conv_w, ssd_conv_b, dt_bias, a_log, d_skip, ssd_norm_w,
          conf_conv_w, conf_conv_b, conf_norm_w, conf_norm_b, w_out, norm_pre_mix, norm_post_mix,
          norm_pre_mlp, norm_post_mlp, w_up, w_down):
    new_ssm, new_sc, new_cc = [], [], []
    for i in range(DEPTH):
        mix, h_new, sbuf, cbuf = mixer(rmsnorm(x, norm_pre_mix[i]), st_ssm[i], st_ssd_conv[i], st_conf[i],
                                       w_in[i], ssd_conv_w[i], ssd_conv_b[i], dt_bias[i], a_log[i], d_skip[i],
                                       ssd_norm_w[i], conf_conv_w[i], conf_conv_b[i], conf_norm_w[i],
                                       conf_norm_b[i], w_out[i])
        x = x + rmsnorm(mix, norm_post_mix[i])
        hid = jnp.square(jax.nn.relu(rmsnorm(x, norm_pre_mlp[i]) @ w_up[i]))
        x = x + rmsnorm(hid @ w_down[i], norm_post_mlp[i])
        new_ssm.append(h_new)
        new_sc.append(sbuf)
        new_cc.append(cbuf)
    return x, jnp.stack(new_ssm), jnp.stack(new_sc), jnp.stack(new_cc)


def setup_inputs(seed: int = 0) -> dict:
    key = jax.random.key(seed)
    ks = jax.random.split(key, 24)
    nrm = lambda k, s, sc: jax.random.normal(k, s, jnp.float32) * sc
    dt0 = jnp.exp(jax.random.uniform(ks[8], (DEPTH, SSD_HEADS), jnp.float32) * (math.log(0.1) - math.log(0.001)) + math.log(0.001))
    dt_bias = dt0 + jnp.log(-jnp.expm1(-dt0))
    a_log = jnp.log(jax.random.uniform(ks[9], (DEPTH, SSD_HEADS), jnp.float32, 1.0, 16.0))
    return {
        "x_prompt": nrm(ks[0], (BATCH, SEQ, D_MODEL), 1.0),
        "x_sample": nrm(ks[1], (DEC_BATCH, DEC_SEQ, D_MODEL), 1.0),
        "state_ssm": nrm(ks[2], (DEPTH, DEC_BATCH, SSD_HEADS, SSD_HEAD_DIM, SSD_STATE), 0.3),
        "state_ssd_conv": nrm(ks[3], (DEPTH, DEC_BATCH, SSD_CONV_W - 1, D_XBC), 1.0),
        "state_conformer_conv": nrm(ks[4], (DEPTH, DEC_BATCH, CONF_CONV_W - 1, D_CONV), 0.5),
        "w_in": nrm(ks[5], (DEPTH, D_MODEL, N_IN), D_MODEL ** -0.5),
        "ssd_conv_w": nrm(ks[6], (DEPTH, SSD_CONV_W, D_XBC), SSD_CONV_W ** -0.5),
        "ssd_conv_b": nrm(ks[7], (DEPTH, D_XBC), 0.02),
        "dt_bias": dt_bias,
        "a_log": a_log,
        "d_skip": 1.0 + nrm(ks[10], (DEPTH, SSD_HEADS), 0.1),
        "ssd_norm_w": 1.0 + nrm(ks[11], (DEPTH, D_SSM), 0.05),
        "conf_conv_w": nrm(ks[12], (DEPTH, CONF_CONV_W, D_CONV), CONF_CONV_W ** -0.5),
        "conf_conv_b": nrm(ks[13], (DEPTH, D_CONV), 0.02),
        "conf_norm_w": 1.0 + nrm(ks[14], (DEPTH, D_CONV), 0.05),
        "conf_norm_b": nrm(ks[15], (DEPTH, D_CONV), 0.02),
        "w_out": nrm(ks[16], (DEPTH, D_MIX, D_MODEL), D_MIX ** -0.5),
        "norm_pre_mix": 1.0 + nrm(ks[17], (DEPTH, D_MODEL), 0.05),
        "norm_post_mix": 1.0 + nrm(ks[18], (DEPTH, D_MODEL), 0.05),
        "norm_pre_mlp": 1.0 + nrm(ks[19], (DEPTH, D_MODEL), 0.05),
        "norm_post_mlp": 1.0 + nrm(ks[20], (DEPTH, D_MODEL), 0.05),
        "w_up": nrm(ks[21], (DEPTH, D_MODEL, D_FF), D_MODEL ** -0.5),
        "w_down": nrm(ks[22], (DEPTH, D_FF, D_MODEL), D_FF ** -0.5),
    }


def reference(x_prompt, x_sample, state_ssm, state_ssd_conv, state_conformer_conv, w_in, ssd_conv_w, ssd_conv_b,
              dt_bias, a_log, d_skip, ssd_norm_w, conf_conv_w, conf_conv_b, conf_norm_w, conf_norm_b, w_out,
              norm_pre_mix, norm_post_mix, norm_pre_mlp, norm_post_mlp, w_up, w_down):
    weights = (w_in, ssd_conv_w, ssd_conv_b, dt_bias, a_log, d_skip, ssd_norm_w, conf_conv_w, conf_conv_b,
               conf_norm_w, conf_norm_b, w_out, norm_pre_mix, norm_post_mix, norm_pre_mlp, norm_post_mlp,
               w_up, w_down)
    dt_p = x_prompt.dtype
    b = x_prompt.shape[0]
    zero_ssm = jnp.zeros((DEPTH, b, SSD_HEADS, SSD_HEAD_DIM, SSD_STATE), dt_p)
    zero_sc = jnp.zeros((DEPTH, b, SSD_CONV_W - 1, D_XBC), dt_p)
    zero_cc = jnp.zeros((DEPTH, b, CONF_CONV_W - 1, D_CONV), dt_p)
    y_prompt, p_ssm, p_sc, p_cc = trunk(x_prompt, zero_ssm, zero_sc, zero_cc, *weights)
    y_sample, s_ssm, s_sc, s_cc = trunk(x_sample, state_ssm, state_ssd_conv, state_conformer_conv, *weights)
    return (y_prompt, y_sample, p_ssm, p_sc, p_cc, s_ssm, s_sc, s_cc)
```

```python
import functools

import jax
import jax.numpy as jnp
from jax import lax
from jax.experimental import pallas as pl
from jax.experimental.pallas import tpu as pltpu

F32 = jnp.float32
BF16 = jnp.bfloat16
HIGHEST = lax.Precision.HIGHEST

D_MODEL = 2048
DEPTH = 4
D_SSM = 2048
D_CONV = 2048
HEAD_DIM = 64
HEADS = 32
GROUPS = 4
HEADS_PER_GROUP = HEADS // GROUPS
GROUP_W = D_SSM // GROUPS
STATE = 128
SSD_CONV_W = 4
D_XBC = D_SSM + 2 * GROUPS * STATE
CONF_W = 31
D_FF = 4 * D_MODEL
EPS = 1e-6

LANES = 128
SUBLANES = 8

N_MAIN = D_SSM + 2 * D_CONV + D_XBC
Z_END = D_SSM
XBC_END = Z_END + D_XBC
DT_END = XBC_END + HEADS

SSD_CHUNK = 128
DEC_SEQ = 8
TILE_BATCH = SSD_CHUNK // DEC_SEQ
STATE_BATCH = 4
CONF_HIST = CONF_W - 1
CONF_PAD = 32
CONF_ROWS = 256
CONF_SUB = 32
STRIP = 512
N_STRIPS = D_CONV // STRIP

VMEM_LIMIT = 56 * 1024 * 1024


def _silu(x):
    return x * (1.0 / (1.0 + jnp.exp(-x)))


def _sigmoid(x):
    return 1.0 / (1.0 + jnp.exp(-x))


def _softplus(x):
    return jnp.maximum(x, 0.0) + jnp.log1p(jnp.exp(-jnp.abs(x)))


def _rms_scale(x):
    return x * lax.rsqrt(jnp.mean(x * x, axis=-1, keepdims=True) + EPS)


def _norm_kernel(*refs, has_y, has_next, has_dt):
    it = iter(refs)
    x_ref = next(it)
    y_ref = next(it) if has_y else None
    wpost_ref = next(it) if has_y else None
    wnext_ref = next(it) if has_next else None
    wdt_ref = next(it) if has_dt else None
    xo_ref = next(it) if has_y else None
    u_ref = next(it) if has_next else None
    dt_ref = next(it) if has_dt else None

    x = x_ref[...]
    if has_y:
        x = x + _rms_scale(y_ref[...]) * wpost_ref[...]
        xo_ref[...] = x
    if has_next:
        u = (_rms_scale(x) * wnext_ref[...]).astype(BF16)
        u_ref[...] = u
        if has_dt:
            dt_ref[...] = jnp.dot(u, wdt_ref[...], preferred_element_type=F32)


def _norm_call(x, y, wpost, post_layer, wnext, wdt, next_layer, *, rows=256):
    t, d = x.shape
    has_y, has_next, has_dt = y is not None, wnext is not None, wdt is not None
    row_spec = pl.BlockSpec((rows, d), lambda i: (i, 0))
    args, in_specs, out_shape, out_specs = [x], [row_spec], [], []
    if has_y:
        args += [y, wpost]
        in_specs += [row_spec, pl.BlockSpec((None, 1, d), lambda i: (post_layer, 0, 0))]
        out_shape.append(jax.ShapeDtypeStruct((t, d), F32))
        out_specs.append(row_spec)
    if has_next:
        args.append(wnext)
        in_specs.append(pl.BlockSpec((None, 1, d), lambda i: (next_layer, 0, 0)))
        out_shape.append(jax.ShapeDtypeStruct((t, d), BF16))
        out_specs.append(row_spec)
    if has_dt:
        args.append(wdt)
        in_specs.append(pl.BlockSpec((None, d, LANES), lambda i: (next_layer, 0, 0)))
        out_shape.append(jax.ShapeDtypeStruct((t, LANES), F32))
        out_specs.append(pl.BlockSpec((rows, LANES), lambda i: (i, 0)))
    return pl.pallas_call(
        functools.partial(_norm_kernel, has_y=has_y, has_next=has_next, has_dt=has_dt),
        grid=(t // rows,),
        in_specs=in_specs,
        out_specs=out_specs,
        out_shape=out_shape,
        compiler_params=pltpu.CompilerParams(
            dimension_semantics=("parallel",), vmem_limit_bytes=VMEM_LIMIT),
        name="norm",
    )(*args)


def _mm_kernel(*refs, n_in, relu2):
    x_refs = refs[:n_in]
    w_refs = refs[n_in:2 * n_in]
    o_ref = refs[2 * n_in]
    acc = jnp.dot(x_refs[0][...], w_refs[0][...], preferred_element_type=F32)
    for xr, wr in zip(x_refs[1:], w_refs[1:]):
        acc = acc + jnp.dot(xr[...], wr[...], preferred_element_type=F32)
    if relu2:
        acc = jnp.maximum(acc, 0.0)
        acc = acc * acc
    o_ref[...] = acc.astype(o_ref.dtype)


def _matmul(xs, w, layer, *, tm, tn, relu2=False, out_dtype=F32, name="matmul"):
    m, k = xs[0].shape
    n = w.shape[-1]
    n_in = len(xs)
    in_specs = [pl.BlockSpec((tm, k), lambda j, i: (i, 0)) for _ in xs]
    in_specs += [pl.BlockSpec((None, k, tn), functools.partial(lambda j, i, p: (layer, p, j), p=p))
                 for p in range(n_in)]
    return pl.pallas_call(
        functools.partial(_mm_kernel, n_in=n_in, relu2=relu2),
        grid=(n // tn, m // tm),
        in_specs=in_specs,
        out_specs=pl.BlockSpec((tm, tn), lambda j, i: (i, j)),
        out_shape=jax.ShapeDtypeStruct((m, n), out_dtype),
        compiler_params=pltpu.CompilerParams(
            dimension_semantics=("parallel", "parallel"), vmem_limit_bytes=VMEM_LIMIT),
        name=name,
    )(*xs, *([w] * n_in))


def _nt_dot(a, b):
    return lax.dot_general(a, b, (((1,), (1,)), ((), ())), preferred_element_type=F32)


def _head_decay_weights(cb, cum, cum_t, dt_t, mask, h):
    seg = cum[:, h:h + 1] - cum_t[h:h + 1, :]
    decay = jnp.exp(jnp.where(mask, seg, -jnp.inf)) * dt_t[h:h + 1, :]
    return (cb * decay).astype(BF16)


def _pair_operands(pair):
    lo = lax.broadcasted_iota(jnp.int32, pair.shape, 1) < HEAD_DIM
    zero = jnp.zeros_like(pair)
    return jnp.concatenate([jnp.where(lo, pair, zero), jnp.where(lo, zero, pair)], axis=0).astype(BF16)


def _gate_norm_store(y, z, nw_ref, y_ref):
    gated = y * _silu(z)
    for g in range(GROUPS):
        cols = slice(g * GROUP_W, (g + 1) * GROUP_W)
        gg = gated[:, cols]
        y_ref[:, cols] = (_rms_scale(gg) * nw_ref[:, cols]).astype(y_ref.dtype)


def _dt_terms(dt_ref, dtb_ref, alog_ref):
    dt = _softplus(dt_ref[...] + dtb_ref[...])
    a = dt * (-jnp.exp(alog_ref[...]))
    return dt, a


def _ssd_prompt_kernel(z_ref, xbc_ref, dt_ref, h0_ref, cb_ref, cw_ref, cbias_ref, dtb_ref, alog_ref,
                       dexp_ref, nw_ref, y_ref, hout_ref, cbout_ref, ext_scr, y_scr, *, n_chunks):
    q = SSD_CHUNK
    c = pl.program_id(1)
    hist0 = SUBLANES - (SSD_CONV_W - 1)

    @pl.when(c == 0)
    def _init():
        hout_ref[...] = h0_ref[...]
        ext_scr[hist0:SUBLANES, :] = cb_ref[...]

    ext_scr[SUBLANES:SUBLANES + q, :] = xbc_ref[...]
    acc = cbias_ref[...] + cw_ref[0:1, :] * ext_scr[hist0:hist0 + q, :]
    for k in range(1, SSD_CONV_W):
        acc = acc + cw_ref[k:k + 1, :] * ext_scr[hist0 + k:hist0 + k + q, :]
    xc = _silu(acc)
    tail = xbc_ref[q - (SSD_CONV_W - 1):q, :]
    ext_scr[hist0:SUBLANES, :] = tail

    @pl.when(c == n_chunks - 1)
    def _conv_state():
        cbout_ref[...] = tail

    dt, a = _dt_terms(dt_ref, dtb_ref, alog_ref)
    row = lax.broadcasted_iota(jnp.int32, (q, q), 0)
    col = lax.broadcasted_iota(jnp.int32, (q, q), 1)
    causal = row >= col
    cum = jnp.dot(causal.astype(F32), a, precision=HIGHEST, preferred_element_type=F32)
    cum_t = cum.T
    dt_t = dt.T
    tot = jnp.broadcast_to(cum_t[:, q - 1:q], (LANES, q))
    wend_t = jnp.exp(tot - cum_t) * dt_t
    cdec = jnp.exp(tot)
    ecum = jnp.exp(cum)

    xs = xc[:, :D_SSM]
    xs_t = xs.T
    row_lo = lax.broadcasted_iota(jnp.int32, (LANES, q), 0) < HEAD_DIM
    for g in range(GROUPS):
        bg = xc[:, D_SSM + g * STATE:D_SSM + (g + 1) * STATE]
        cg = xc[:, D_SSM + GROUPS * STATE + g * STATE:D_SSM + GROUPS * STATE + (g + 1) * STATE]
        bg16 = bg.astype(BF16)
        cb = _nt_dot(cg.astype(BF16), bg16)
        for k in range(HEADS_PER_GROUP // 2):
            h1 = g * HEADS_PER_GROUP + 2 * k
            h2 = h1 + 1
            pr = h1 * HEAD_DIM
            w12 = jnp.concatenate([_head_decay_weights(cb, cum, cum_t, dt_t, causal, h1),
                                   _head_decay_weights(cb, cum, cum_t, dt_t, causal, h2)], axis=1)
            y_pair = jnp.dot(w12, _pair_operands(xs[:, pr:pr + LANES]), preferred_element_type=F32)
            hp = hout_ref[pr:pr + LANES, :]
            zero = jnp.zeros_like(hp)
            hbd = jnp.concatenate([jnp.where(row_lo, hp, zero), jnp.where(row_lo, zero, hp)],
                                  axis=1).astype(BF16)
            cs12 = jnp.concatenate([(cg * ecum[:, h1:h1 + 1]).astype(BF16),
                                    (cg * ecum[:, h2:h2 + 1]).astype(BF16)], axis=1)
            y_scr[:, pr:pr + LANES] = y_pair + _nt_dot(cs12, hbd)
            wrow = jnp.where(row_lo, wend_t[h1:h1 + 1, :], wend_t[h2:h2 + 1, :])
            xw = (xs_t[pr:pr + LANES, :] * wrow).astype(BF16)
            s_new = jnp.dot(xw, bg16, preferred_element_type=F32)
            drow = jnp.where(row_lo, cdec[h1:h1 + 1, :], cdec[h2:h2 + 1, :])
            hout_ref[pr:pr + LANES, :] = hp * drow + s_new

    y = y_scr[...] + dexp_ref[...] * xs
    _gate_norm_store(y, z_ref[...], nw_ref, y_ref)


def _ssd_prompt(proj, dtraw, h0, cb0, layer, cw, cbias, dtb, alog, dexp, nw, *, batch, seq):
    q = SSD_CHUNK
    nc = seq // q
    xbc_blk = (N_MAIN - D_XBC) // D_XBC
    rowmap = lambda b, c: (b * nc + c, 0)
    per_layer = lambda b, c: (layer, 0, 0)
    per_batch = lambda b, c: (b, 0, 0)
    return pl.pallas_call(
        functools.partial(_ssd_prompt_kernel, n_chunks=nc),
        grid=(batch, nc),
        in_specs=[
            pl.BlockSpec((q, D_SSM), rowmap),
            pl.BlockSpec((q, D_XBC), lambda b, c: (b * nc + c, xbc_blk)),
            pl.BlockSpec((q, LANES), rowmap),
            pl.BlockSpec((None, D_SSM, STATE), per_batch),
            pl.BlockSpec((None, SSD_CONV_W - 1, D_XBC), per_batch),
            pl.BlockSpec((None, SSD_CONV_W, D_XBC), per_layer),
            pl.BlockSpec((None, 1, D_XBC), per_layer),
            pl.BlockSpec((None, 1, LANES), per_layer),
            pl.BlockSpec((None, 1, LANES), per_layer),
            pl.BlockSpec((None, 1, D_SSM), per_layer),
            pl.BlockSpec((None, 1, D_SSM), per_layer),
        ],
        out_specs=[
            pl.BlockSpec((q, D_SSM), rowmap),
            pl.BlockSpec((None, D_SSM, STATE), per_batch),
            pl.BlockSpec((None, SSD_CONV_W - 1, D_XBC), per_batch),
        ],
        out_shape=[
            jax.ShapeDtypeStruct((batch * seq, D_SSM), BF16),
            jax.ShapeDtypeStruct((batch, D_SSM, STATE), F32),
            jax.ShapeDtypeStruct((batch, SSD_CONV_W - 1, D_XBC), F32),
        ],
        scratch_shapes=[
            pltpu.VMEM((SUBLANES + q, D_XBC), F32),
            pltpu.VMEM((q, D_SSM), F32),
        ],
        compiler_params=pltpu.CompilerParams(
            dimension_semantics=("parallel", "arbitrary"), vmem_limit_bytes=VMEM_LIMIT),
        name="ssd_prompt",
    )(proj, proj, dtraw, h0, cb0, cw, cbias, dtb, alog, dexp, nw)


def _ssd_sample_kernel(*refs, n_sub, has_stack):
    (z_ref, xbc_ref, dt_ref, h0_ref, cb_ref, cw_ref, cbias_ref, dtb_ref, alog_ref,
     dexp_ref, nw_ref, e_ref) = refs[:12]
    (y_ref, hout_ref, cbout_ref,
     ext_scr, y_scr, xwt_scr, b_scr, c_scr, ecx_scr, cd_scr) = refs[12 + int(has_stack):]
    q = SSD_CHUNK
    tb, sb, ls = TILE_BATCH, STATE_BATCH, DEC_SEQ
    sub = pl.program_id(1)
    hist0 = SUBLANES - (SSD_CONV_W - 1)

    @pl.when(sub == 0)
    def _token_space():
        ext_scr[:, hist0:SUBLANES, :] = cb_ref[...]
        ext_scr[:, SUBLANES:SUBLANES + ls, :] = xbc_ref[...].reshape(tb, ls, D_XBC)
        cbout_ref[...] = ext_scr[:, SUBLANES + ls - (SSD_CONV_W - 1):SUBLANES + ls, :]
        acc = cbias_ref[...] + cw_ref[0:1, :] * ext_scr[:, hist0:hist0 + ls, :]
        for k in range(1, SSD_CONV_W):
            acc = acc + cw_ref[k:k + 1, :] * ext_scr[:, hist0 + k:hist0 + k + ls, :]
        xc = _silu(acc).reshape(q, D_XBC)

        dt, a = _dt_terms(dt_ref, dtb_ref, alog_ref)
        row = lax.broadcasted_iota(jnp.int32, (q, q), 0)
        col = lax.broadcasted_iota(jnp.int32, (q, q), 1)
        same = (row // ls) == (col // ls)
        mask = jnp.logical_and(same, row >= col)
        cum = jnp.dot(mask.astype(F32), a, precision=HIGHEST, preferred_element_type=F32)
        tot = jnp.dot(same.astype(F32), a, precision=HIGHEST, preferred_element_type=F32)
        cum_t = cum.T
        dt_t = dt.T
        tot_t = tot.T
        wend_t = jnp.exp(tot_t - cum_t) * dt_t
        cdec_t = jnp.exp(tot_t)
        for b in range(tb):
            cd_scr[b] = jnp.broadcast_to(cdec_t[:, b * ls:b * ls + 1], (LANES, STATE))
        ecx_scr[...] = jnp.dot(jnp.exp(cum), e_ref[...], precision=HIGHEST, preferred_element_type=F32)

        xs = xc[:, :D_SSM]
        xs_t = xs.T
        for h in range(HEADS):
            rows = slice(h * HEAD_DIM, (h + 1) * HEAD_DIM)
            xwt_scr[rows, :] = (xs_t[rows, :] * wend_t[h:h + 1, :]).astype(BF16)
        b_scr[...] = xc[:, D_SSM:D_SSM + GROUPS * STATE].astype(BF16)
        c_scr[...] = xc[:, D_SSM + GROUPS * STATE:]
        for g in range(GROUPS):
            bg16 = xc[:, D_SSM + g * STATE:D_SSM + (g + 1) * STATE].astype(BF16)
            cg16 = xc[:, D_SSM + GROUPS * STATE + g * STATE:
                      D_SSM + GROUPS * STATE + (g + 1) * STATE].astype(BF16)
            cb = _nt_dot(cg16, bg16)
            for k in range(HEADS_PER_GROUP // 2):
                h1 = g * HEADS_PER_GROUP + 2 * k
                pr = h1 * HEAD_DIM
                w12 = jnp.concatenate([_head_decay_weights(cb, cum, cum_t, dt_t, mask, h1),
                                       _head_decay_weights(cb, cum, cum_t, dt_t, mask, h1 + 1)], axis=1)
                pair = xs[:, pr:pr + LANES]
                y_scr[:, pr:pr + LANES] = (
                    jnp.dot(w12, _pair_operands(pair), preferred_element_type=F32)
                    + dexp_ref[:, pr:pr + LANES] * pair)

    nrow = sb * ls
    rows = pl.ds(pl.multiple_of(sub * nrow, nrow), nrow)
    local_seq = lax.broadcasted_iota(jnp.int32, (nrow, STATE), 0) // ls
    tile_seq = lax.broadcasted_iota(jnp.int32, (q, STATE), 0) // ls
    for g in range(GROUPS):
        gcols = slice(g * GROUP_W, (g + 1) * GROUP_W)
        scols = slice(g * STATE, (g + 1) * STATE)
        cg = c_scr[rows, scols]
        zero = jnp.zeros_like(cg)
        lhs = jnp.concatenate([jnp.where(local_seq == b, cg, zero) for b in range(sb)],
                              axis=1).astype(BF16)
        hcat = jnp.concatenate([h0_ref[b, gcols, :].astype(BF16) for b in range(sb)], axis=1)
        y_off = _nt_dot(lhs, hcat)
        y_scr[rows, gcols] = y_scr[rows, gcols] + y_off * ecx_scr[rows, gcols]

        bg = b_scr[:, scols]
        bzero = jnp.zeros_like(bg)
        bsel = jnp.concatenate([jnp.where(tile_seq == sub * sb + b, bg, bzero) for b in range(sb)],
                               axis=1)
        s_new = jnp.dot(xwt_scr[gcols, :], bsel, preferred_element_type=F32)
        for b in range(sb):
            cdb = cd_scr[sub * sb + b]
            for hl in range(HEADS_PER_GROUP):
                h = g * HEADS_PER_GROUP + hl
                hrows = slice(h * HEAD_DIM, (h + 1) * HEAD_DIM)
                hout_ref[b, hrows, :] = (
                    h0_ref[b, hrows, :] * cdb[h:h + 1, :]
                    + s_new[hl * HEAD_DIM:(hl + 1) * HEAD_DIM, b * STATE:(b + 1) * STATE])

    @pl.when(sub == n_sub - 1)
    def _finish():
        _gate_norm_store(y_scr[...], z_ref[...], nw_ref, y_ref)


def _ssd_sample(proj, dtraw, h0, cb0, h_stack, layer, cw, cbias, dtb, alog, dexp, nw, expand, *, batch, row0):
    q = SSD_CHUNK
    tb, sb = TILE_BATCH, STATE_BATCH
    n_sub = tb // sb
    n_tiles = batch // tb
    rb0 = row0 // q
    xbc_blk = (N_MAIN - D_XBC) // D_XBC
    rowmap = lambda i, s: (rb0 + i, 0)
    per_layer = lambda i, s: (layer, 0, 0)
    stacked = [] if h_stack is None else [h_stack]
    n_fixed = 12
    return pl.pallas_call(
        functools.partial(_ssd_sample_kernel, n_sub=n_sub, has_stack=bool(stacked)),
        grid=(n_tiles, n_sub),
        input_output_aliases={n_fixed: 1} if stacked else {},
        in_specs=[
            pl.BlockSpec((q, D_SSM), rowmap),
            pl.BlockSpec((q, D_XBC), lambda i, s: (rb0 + i, xbc_blk)),
            pl.BlockSpec((q, LANES), rowmap),
            pl.BlockSpec((None, sb, D_SSM, STATE), lambda i, s: (layer, i * n_sub + s, 0, 0)),
            pl.BlockSpec((None, tb, SSD_CONV_W - 1, D_XBC), lambda i, s: (layer, i, 0, 0)),
            pl.BlockSpec((None, SSD_CONV_W, D_XBC), per_layer),
            pl.BlockSpec((None, 1, D_XBC), per_layer),
            pl.BlockSpec((None, 1, LANES), per_layer),
            pl.BlockSpec((None, 1, LANES), per_layer),
            pl.BlockSpec((None, 1, D_SSM), per_layer),
            pl.BlockSpec((None, 1, D_SSM), per_layer),
            pl.BlockSpec((LANES, D_SSM), lambda i, s: (0, 0)),
        ] + [pl.BlockSpec(memory_space=pl.ANY) for _ in stacked],
        out_specs=[
            pl.BlockSpec((q, D_SSM), lambda i, s: (i, 0)),
            pl.BlockSpec((None, sb, D_SSM, STATE), lambda i, s: (layer, i * n_sub + s, 0, 0)),
            pl.BlockSpec((tb, SSD_CONV_W - 1, D_XBC), lambda i, s: (i, 0, 0)),
        ],
        out_shape=[
            jax.ShapeDtypeStruct((batch * DEC_SEQ, D_SSM), BF16),
            jax.ShapeDtypeStruct(h0.shape, F32),
            jax.ShapeDtypeStruct((batch, SSD_CONV_W - 1, D_XBC), F32),
        ],
        scratch_shapes=[
            pltpu.VMEM((tb, SUBLANES + DEC_SEQ, D_XBC), F32),
            pltpu.VMEM((q, D_SSM), F32),
            pltpu.VMEM((D_SSM, q), BF16),
            pltpu.VMEM((q, GROUPS * STATE), BF16),
            pltpu.VMEM((q, GROUPS * STATE), F32),
            pltpu.VMEM((q, D_SSM), F32),
            pltpu.VMEM((tb, LANES, STATE), F32),
        ],
        compiler_params=pltpu.CompilerParams(
            dimension_semantics=("parallel", "arbitrary"), vmem_limit_bytes=VMEM_LIMIT),
        name="ssd_sample",
    )(proj, proj, dtraw, h0, cb0, cw, cbias, dtb, alog, dexp, nw, expand, *stacked)


def _layernorm_silu_store(vc_strips, lnw_ref, lnb_ref, y_ref):
    total = vc_strips[0].sum(axis=-1, keepdims=True)
    for v in vc_strips[1:]:
        total = total + v.sum(axis=-1, keepdims=True)
    mu = total * (1.0 / D_CONV)
    sq = None
    for v in vc_strips:
        d = v - mu
        part = (d * d).sum(axis=-1, keepdims=True)
        sq = part if sq is None else sq + part
    rstd = lax.rsqrt(sq * (1.0 / D_CONV) + EPS)
    for s, v in enumerate(vc_strips):
        cols = slice(s * STRIP, (s + 1) * STRIP)
        o = ((v - mu) * rstd) * lnw_ref[:, cols] + lnb_ref[:, cols]
        y_ref[:, cols] = _silu(o).astype(y_ref.dtype)


def _conf_prompt_kernel(a_ref, b_ref, buf_ref, w_ref, bias_ref, lnw_ref, lnb_ref, y_ref, bufout_ref,
                        ext_scr, vc_scr, *, n_steps):
    r = CONF_ROWS
    c = pl.program_id(1)
    lead = CONF_PAD - CONF_HIST

    @pl.when(c == 0)
    def _init():
        for s in range(N_STRIPS):
            ext_scr[s, 0:CONF_PAD, :] = buf_ref[:, s * STRIP:(s + 1) * STRIP]

    for s in range(N_STRIPS):
        cols = slice(s * STRIP, (s + 1) * STRIP)
        ext_scr[s, CONF_PAD:CONF_PAD + r, :] = a_ref[:, cols] * _sigmoid(b_ref[:, cols])

    def strip_body(s, carry):
        for rb in range(r // CONF_SUB):
            base = rb * CONF_SUB + lead
            acc = bias_ref[s] + w_ref[s, 0:1, :] * ext_scr[s, base:base + CONF_SUB, :]
            for k in range(1, CONF_W):
                acc = acc + w_ref[s, k:k + 1, :] * ext_scr[s, base + k:base + k + CONF_SUB, :]
            vc_scr[s, rb * CONF_SUB:(rb + 1) * CONF_SUB, :] = acc
        return carry

    lax.fori_loop(0, N_STRIPS, strip_body, 0)

    @pl.when(c == n_steps - 1)
    def _state():
        for s in range(N_STRIPS):
            bufout_ref[:, s * STRIP:(s + 1) * STRIP] = ext_scr[s, r + lead:r + CONF_PAD, :]

    for s in range(N_STRIPS):
        ext_scr[s, 0:CONF_PAD, :] = ext_scr[s, r:r + CONF_PAD, :]

    _layernorm_silu_store([vc_scr[s] for s in range(N_STRIPS)], lnw_ref, lnb_ref, y_ref)


def _conf_prompt(proj, buf32, layer, w_strips, bias_strips, lnw, lnb, *, batch, seq):
    r = CONF_ROWS
    ns = seq // r
    per_layer = lambda b, c: (layer, 0, 0)
    return pl.pallas_call(
        functools.partial(_conf_prompt_kernel, n_steps=ns),
        grid=(batch, ns),
        in_specs=[
            pl.BlockSpec((r, D_CONV), lambda b, c: (b * ns + c, 1)),
            pl.BlockSpec((r, D_CONV), lambda b, c: (b * ns + c, 2)),
            pl.BlockSpec((None, CONF_PAD, D_CONV), lambda b, c: (b, 0, 0)),
            pl.BlockSpec((None, N_STRIPS, CONF_W, STRIP), lambda b, c: (layer, 0, 0, 0)),
            pl.BlockSpec((None, N_STRIPS, 1, STRIP), lambda b, c: (layer, 0, 0, 0)),
            pl.BlockSpec((None, 1, D_CONV), per_layer),
            pl.BlockSpec((None, 1, D_CONV), per_layer),
        ],
        out_specs=[
            pl.BlockSpec((r, D_CONV), lambda b, c: (b * ns + c, 0)),
            pl.BlockSpec((None, CONF_HIST, D_CONV), lambda b, c: (b, 0, 0)),
        ],
        out_shape=[
            jax.ShapeDtypeStruct((batch * seq, D_CONV), BF16),
            jax.ShapeDtypeStruct((batch, CONF_HIST, D_CONV), F32),
        ],
        scratch_shapes=[
            pltpu.VMEM((N_STRIPS, CONF_PAD + r, STRIP), F32),
            pltpu.VMEM((N_STRIPS, r, STRIP), F32),
        ],
        compiler_params=pltpu.CompilerParams(
            dimension_semantics=("parallel", "arbitrary"), vmem_limit_bytes=VMEM_LIMIT),
        name="conf_prompt",
    )(proj, proj, buf32, w_strips, bias_strips, lnw, lnb)


def _conf_sample_kernel(*refs, has_stack):
    a_ref, b_ref, buf_ref, w_ref, bias_ref, lnw_ref, lnb_ref = refs[:7]
    y_ref, bufout_ref, ext_scr, vc_scr = refs[7 + int(has_stack):]
    tb, ls = TILE_BATCH, DEC_SEQ
    lead = CONF_PAD - CONF_HIST
    v = a_ref[...] * _sigmoid(b_ref[...])
    ext_scr[:, lead:CONF_PAD, :] = buf_ref[...]
    ext_scr[:, CONF_PAD:CONF_PAD + ls, :] = v.reshape(tb, ls, D_CONV)
    bufout_ref[...] = ext_scr[:, lead + ls:CONF_PAD + ls, :]

    def seq_body(b, carry):
        acc = bias_ref[...] + w_ref[0:1, :] * ext_scr[b, lead:lead + ls, :]
        for k in range(1, CONF_W):
            acc = acc + w_ref[k:k + 1, :] * ext_scr[b, lead + k:lead + k + ls, :]
        vc_scr[b] = acc
        return carry

    lax.fori_loop(0, tb, seq_body, 0)
    vc = vc_scr[...].reshape(tb * ls, D_CONV)
    _layernorm_silu_store([vc[:, s * STRIP:(s + 1) * STRIP] for s in range(N_STRIPS)],
                          lnw_ref, lnb_ref, y_ref)


def _conf_sample(proj, buf, buf_stack, layer, w, bias, lnw, lnb, *, batch, row0):
    tb, ls = TILE_BATCH, DEC_SEQ
    rows = tb * ls
    rb0 = row0 // rows
    per_layer = lambda i: (layer, 0, 0)
    stacked = [] if buf_stack is None else [buf_stack]
    n_fixed = 7
    return pl.pallas_call(
        functools.partial(_conf_sample_kernel, has_stack=bool(stacked)),
        grid=(batch // tb,),
        input_output_aliases={n_fixed: 1} if stacked else {},
        in_specs=[
            pl.BlockSpec((rows, D_CONV), lambda i: (rb0 + i, 1)),
            pl.BlockSpec((rows, D_CONV), lambda i: (rb0 + i, 2)),
            pl.BlockSpec((None, tb, CONF_HIST, D_CONV), lambda i: (layer, i, 0, 0)),
            pl.BlockSpec((None, CONF_W, D_CONV), per_layer),
            pl.BlockSpec((None, 1, D_CONV), per_layer),
            pl.BlockSpec((None, 1, D_CONV), per_layer),
            pl.BlockSpec((None, 1, D_CONV), per_layer),
        ] + [pl.BlockSpec(memory_space=pl.ANY) for _ in stacked],
        out_specs=[
            pl.BlockSpec((rows, D_CONV), lambda i: (i, 0)),
            pl.BlockSpec((None, tb, CONF_HIST, D_CONV), lambda i: (layer, i, 0, 0)),
        ],
        out_shape=[
            jax.ShapeDtypeStruct((batch * ls, D_CONV), BF16),
            jax.ShapeDtypeStruct(buf.shape, F32),
        ],
        scratch_shapes=[
            pltpu.VMEM((tb, CONF_PAD + ls, D_CONV), F32),
            pltpu.VMEM((tb, ls, D_CONV), F32),
        ],
        compiler_params=pltpu.CompilerParams(
            dimension_semantics=("parallel",), vmem_limit_bytes=VMEM_LIMIT),
        name="conf_sample",
    )(proj, proj, buf, w, bias, lnw, lnb, *stacked)


def kernel(x_prompt, x_sample, state_ssm, state_ssd_conv, state_conformer_conv, w_in, ssd_conv_w, ssd_conv_b,
           dt_bias, a_log, d_skip, ssd_norm_w, conf_conv_w, conf_conv_b, conf_norm_w, conf_norm_b, w_out,
           norm_pre_mix, norm_post_mix, norm_pre_mlp, norm_post_mlp, w_up, w_down):
    pb, pl_len, d = x_prompt.shape
    sbatch, s_len, _ = x_sample.shape
    tp = pb * pl_len
    ts = sbatch * s_len
    depth = w_in.shape[0]

    w_main = jnp.concatenate([w_in[..., :Z_END], w_in[..., DT_END:], w_in[..., Z_END:XBC_END]],
                             axis=-1).astype(BF16)
    w_dt = jnp.pad(w_in[..., XBC_END:DT_END], ((0, 0), (0, 0), (0, LANES - HEADS))).astype(BF16)
    w_out16 = w_out.astype(BF16)
    w_up16 = w_up.astype(BF16)
    w_down16 = w_down.astype(BF16)

    vec = lambda a: a.reshape(depth, 1, a.shape[-1])
    pad_heads = lambda a: jnp.pad(a, ((0, 0), (0, LANES - HEADS))).reshape(depth, 1, LANES)
    dtb = pad_heads(dt_bias)
    alog = pad_heads(a_log)
    dexp = vec(jnp.repeat(d_skip, HEAD_DIM, axis=-1))
    cbias = vec(ssd_conv_b)
    ssd_nw = vec(ssd_norm_w)
    conf_w_strips = conf_conv_w.reshape(depth, CONF_W, N_STRIPS, STRIP).transpose(0, 2, 1, 3)
    conf_b_strips = conf_conv_b.reshape(depth, N_STRIPS, 1, STRIP)
    conf_b = vec(conf_conv_b)
    lnw = vec(conf_norm_w)
    lnb = vec(conf_norm_b)
    n_pre_mix, n_post_mix = vec(norm_pre_mix), vec(norm_post_mix)
    n_pre_mlp, n_post_mlp = vec(norm_pre_mlp), vec(norm_post_mlp)
    expand = (lax.broadcasted_iota(jnp.int32, (LANES, D_SSM), 1) // HEAD_DIM
              == lax.broadcasted_iota(jnp.int32, (LANES, D_SSM), 0)).astype(F32)

    h0_sample = state_ssm.reshape(depth, sbatch, D_SSM, STATE)
    h0_prompt = jnp.zeros((pb, D_SSM, STATE), F32)
    cb0_prompt = jnp.zeros((pb, SSD_CONV_W - 1, D_XBC), F32)
    conf0_prompt = jnp.zeros((pb, CONF_PAD, D_CONV), F32)

    x = jnp.concatenate([x_prompt.reshape(tp, d), x_sample.reshape(ts, d)], axis=0)
    u, dtraw = _norm_call(x, None, None, 0, n_pre_mix, w_dt, 0)

    p_ssm, p_sc, p_cc, s_sc = [], [], [], []
    s_ssm = s_cc = None
    for i in range(depth):
        proj = _matmul([u], w_main, i, tm=1024, tn=1024, name="in_proj")
        ssd_p, h_p, sc_p = _ssd_prompt(proj, dtraw, h0_prompt, cb0_prompt, i, ssd_conv_w, cbias, dtb, alog,
                                       dexp, ssd_nw, batch=pb, seq=pl_len)
        ssd_s, s_ssm, sc_s = _ssd_sample(proj, dtraw, h0_sample, state_ssd_conv, s_ssm, i, ssd_conv_w, cbias,
                                         dtb, alog, dexp, ssd_nw, expand, batch=sbatch, row0=tp)
        conf_p, cc_p = _conf_prompt(proj, conf0_prompt, i, conf_w_strips, conf_b_strips, lnw, lnb,
                                    batch=pb, seq=pl_len)
        conf_s, s_cc = _conf_sample(proj, state_conformer_conv, s_cc, i, conf_conv_w, conf_b, lnw, lnb,
                                    batch=sbatch, row0=tp)
        ssd_all = jnp.concatenate([ssd_p, ssd_s], axis=0)
        conf_all = jnp.concatenate([conf_p, conf_s], axis=0)
        mix = _matmul([ssd_all, conf_all], w_out16, i, tm=1024, tn=512, name="out_proj")
        x, u2 = _norm_call(x, mix, n_post_mix, i, n_pre_mlp, None, i)
        hid = _matmul([u2], w_up16, i, tm=1024, tn=1024, relu2=True, out_dtype=BF16, name="up_proj")
        down = _matmul([hid], w_down16, i, tm=512, tn=512, name="down_proj")
        if i + 1 < depth:
            x, u, dtraw = _norm_call(x, down, n_post_mlp, i, n_pre_mix, w_dt, i + 1)
        else:
            (x,) = _norm_call(x, down, n_post_mlp, i, None, None, i)
        p_ssm.append(h_p)
        p_sc.append(sc_p)
        p_cc.append(cc_p)
        s_sc.append(sc_s)

    y_prompt = x[:tp].reshape(pb, pl_len, d)
    y_sample = x[tp:].reshape(sbatch, s_len, d)
    state_shape = lambda b: (depth, b, HEADS, HEAD_DIM, STATE)
    return (y_prompt, y_sample,
            jnp.stack(p_ssm).reshape(state_shape(pb)), jnp.stack(p_sc), jnp.stack(p_cc),
            s_ssm.reshape(state_shape(sbatch)), jnp.stack(s_sc), s_cc)
```

```python
import functools

import jax
import jax.numpy as jnp
from jax import lax
from jax.experimental import pallas as pl
from jax.experimental.pallas import tpu as pltpu

F32 = jnp.float32
BF16 = jnp.bfloat16
HIGHEST = lax.Precision.HIGHEST

D_MODEL = 2048
DEPTH = 4
D_SSM = 2048
D_CONV = 2048
HEAD_DIM = 64
HEADS = 32
GROUPS = 4
HEADS_PER_GROUP = HEADS // GROUPS
GROUP_W = D_SSM // GROUPS
STATE = 128
SSD_CONV_W = 4
SSD_HIST = SSD_CONV_W - 1
D_XBC = D_SSM + 2 * GROUPS * STATE
CONF_W = 31
CONF_HIST = CONF_W - 1
D_FF = 4 * D_MODEL
EPS = 1e-6

LANES = 128
SUBLANES = 8
TILE_GROUP = 4

N_MAIN = D_SSM + 2 * D_CONV + D_XBC
Z_END = D_SSM
XBC_END = Z_END + D_XBC
DT_END = XBC_END + HEADS

SSD_CHUNK = 128
DEC_SEQ = 8
TILE_BATCH = SSD_CHUNK // DEC_SEQ
STATE_BATCH = 4
CONF_ROWS = 256
CONF_PAD = 32
CONV_TILES = D_CONV // LANES
XBC_TILES = D_XBC // LANES
CONV_SLABS = CONV_TILES // TILE_GROUP
XBC_SLABS = XBC_TILES // TILE_GROUP

VMEM_LIMIT = 56 * 1024 * 1024


def _sigmoid(x):
    return 0.5 + 0.5 * jnp.tanh(0.5 * x)


def _silu(x):
    h = 0.5 * x
    return h + h * jnp.tanh(h)


def _softplus(x):
    return jnp.maximum(x, 0.0) + jnp.log1p(jnp.exp(-jnp.abs(x)))


def _rms_scale(x):
    return x * lax.rsqrt(jnp.mean(x * x, axis=-1, keepdims=True) + EPS)


def _tile(t):
    return t // TILE_GROUP, t % TILE_GROUP, slice(t * LANES, (t + 1) * LANES)


def _rows(j, start, n):
    return pl.ds(TILE_GROUP * start + j, n, stride=TILE_GROUP)


def _norm_kernel(*refs, has_y, has_next, has_dt):
    it = iter(refs)
    x_ref = next(it)
    y_ref = next(it) if has_y else None
    wpost_ref = next(it) if has_y else None
    wnext_ref = next(it) if has_next else None
    wdt_ref = next(it) if has_dt else None
    xo_ref = next(it) if has_y else None
    u_ref = next(it) if has_next else None
    dt_ref = next(it) if has_dt else None

    x = x_ref[...]
    if has_y:
        x = x + _rms_scale(y_ref[...]) * wpost_ref[...]
        xo_ref[...] = x
    if has_next:
        u = (_rms_scale(x) * wnext_ref[...]).astype(BF16)
        u_ref[...] = u
        if has_dt:
            dt_ref[...] = jnp.dot(u, wdt_ref[...], preferred_element_type=F32)


def _norm_call(x, y, wpost, post_layer, wnext, wdt, next_layer, *, rows=256):
    t, d = x.shape
    has_y, has_next, has_dt = y is not None, wnext is not None, wdt is not None
    row_spec = pl.BlockSpec((rows, d), lambda i: (i, 0))
    args, in_specs, out_shape, out_specs = [x], [row_spec], [], []
    if has_y:
        args += [y, wpost]
        in_specs += [row_spec, pl.BlockSpec((None, 1, d), lambda i: (post_layer, 0, 0))]
        out_shape.append(jax.ShapeDtypeStruct((t, d), F32))
        out_specs.append(row_spec)
    if has_next:
        args.append(wnext)
        in_specs.append(pl.BlockSpec((None, 1, d), lambda i: (next_layer, 0, 0)))
        out_shape.append(jax.ShapeDtypeStruct((t, d), BF16))
        out_specs.append(row_spec)
    if has_dt:
        args.append(wdt)
        in_specs.append(pl.BlockSpec((None, d, LANES), lambda i: (next_layer, 0, 0)))
        out_shape.append(jax.ShapeDtypeStruct((t, LANES), F32))
        out_specs.append(pl.BlockSpec((rows, LANES), lambda i: (i, 0)))
    return pl.pallas_call(
        functools.partial(_norm_kernel, has_y=has_y, has_next=has_next, has_dt=has_dt),
        grid=(t // rows,),
        in_specs=in_specs,
        out_specs=out_specs,
        out_shape=out_shape,
        compiler_params=pltpu.CompilerParams(
            dimension_semantics=("parallel",), vmem_limit_bytes=VMEM_LIMIT),
        name="norm",
    )(*args)


def _mm_kernel(*refs, n_in, relu2):
    x_refs = refs[:n_in]
    w_refs = refs[n_in:2 * n_in]
    o_ref = refs[2 * n_in]
    acc = jnp.dot(x_refs[0][...], w_refs[0][...], preferred_element_type=F32)
    for xr, wr in zip(x_refs[1:], w_refs[1:]):
        acc = acc + jnp.dot(xr[...], wr[...], preferred_element_type=F32)
    if relu2:
        acc = jnp.maximum(acc, 0.0)
        acc = acc * acc
    o_ref[...] = acc.astype(o_ref.dtype)


def _matmul(xs, w, layer, *, tm, tn, relu2=False, out_dtype=F32, name="matmul"):
    m, k = xs[0].shape
    n = w.shape[-1]
    n_in = len(xs)
    in_specs = [pl.BlockSpec((tm, k), lambda j, i: (i, 0)) for _ in xs]
    in_specs += [pl.BlockSpec((None, k, tn), functools.partial(lambda j, i, p: (layer, p, j), p=p))
                 for p in range(n_in)]
    return pl.pallas_call(
        functools.partial(_mm_kernel, n_in=n_in, relu2=relu2),
        grid=(n // tn, m // tm),
        in_specs=in_specs,
        out_specs=pl.BlockSpec((tm, tn), lambda j, i: (i, j)),
        out_shape=jax.ShapeDtypeStruct((m, n), out_dtype),
        compiler_params=pltpu.CompilerParams(
            dimension_semantics=("parallel", "parallel"), vmem_limit_bytes=VMEM_LIMIT),
        name=name,
    )(*xs, *([w] * n_in))


def _nt_dot(a, b):
    return lax.dot_general(a, b, (((1,), (1,)), ((), ())), preferred_element_type=F32)


LOG2E = 1.4426950408889634


def _head_decay_weights(cb, cum2, cum2_t, dt_t, mask, h):
    seg2 = cum2[:, h:h + 1] - cum2_t[h:h + 1, :]
    decay = jnp.exp2(jnp.where(mask, seg2, -jnp.inf)) * dt_t[h:h + 1, :]
    return (cb * decay).astype(BF16)


def _pair_y_diag(cb, cum2, cum2_t, dt_t, mask, h1, pair):
    q = pair.shape[0]
    w_stack = jnp.concatenate([_head_decay_weights(cb, cum2, cum2_t, dt_t, mask, h1),
                               _head_decay_weights(cb, cum2, cum2_t, dt_t, mask, h1 + 1)], axis=0)
    yy = jnp.dot(w_stack, pair.astype(BF16), preferred_element_type=F32)
    lo = lax.broadcasted_iota(jnp.int32, pair.shape, 1) < HEAD_DIM
    return jnp.where(lo, yy[:q], yy[q:])


def _gate_norm_store(y_scr, z_ref, nw_ref, y_ref):
    for g in range(GROUPS):
        cols = slice(g * GROUP_W, (g + 1) * GROUP_W)
        gated = y_scr[:, cols] * _silu(z_ref[:, cols])
        y_ref[:, cols] = (_rms_scale(gated) * nw_ref[:, cols]).astype(y_ref.dtype)


def _expand_heads(v, e3_ref):
    hi = v.astype(BF16)
    r1 = v - hi.astype(F32)
    mid = r1.astype(BF16)
    lo = (r1 - mid.astype(F32)).astype(BF16)
    return jnp.dot(jnp.concatenate([hi, mid, lo], axis=1), e3_ref[...], preferred_element_type=F32)


def _dt_terms(dt_ref, dtb_ref, alog_ref):
    dt = _softplus(dt_ref[...] + dtb_ref[...])
    a = dt * (-jnp.exp(alog_ref[...]))
    return dt, a


def _ssd_prompt_kernel(z_ref, xbc_ref, dt_ref, h0_ref, cb_ref, cw_ref, cbias_ref, dtb_ref, alog_ref,
                       dexp_ref, nw_ref, e_ref, y_ref, hout_ref, cbout_ref,
                       ext_scr, xc_scr, y_scr, ecx_scr, *, n_chunks):
    q = SSD_CHUNK
    half = HEAD_DIM
    c = pl.program_id(1)

    @pl.when(c == 0)
    def _init():
        hout_ref[...] = h0_ref[...]
        for t in range(XBC_TILES):
            slab, j, cols = _tile(t)
            ext_scr[slab, _rows(j, 0, SUBLANES), :] = cb_ref[:, cols]

    dt, a = _dt_terms(dt_ref, dtb_ref, alog_ref)
    row = lax.broadcasted_iota(jnp.int32, (q, q), 0)
    col = lax.broadcasted_iota(jnp.int32, (q, q), 1)
    causal = row >= col
    cum = jnp.dot(causal.astype(F32), a, precision=HIGHEST, preferred_element_type=F32)
    cum_t = cum.T
    dt_t = dt.T
    tot = jnp.broadcast_to(cum_t[:, q - 1:q], (LANES, q))
    wend_t = jnp.exp(tot - cum_t) * dt_t
    cdec = jnp.exp(tot)
    ecx_scr[...] = _expand_heads(jnp.exp(cum), e_ref)
    cum2 = cum * LOG2E
    cum2_t = cum_t * LOG2E

    for t in range(XBC_TILES):
        slab, j, cols = _tile(t)
        ext_scr[slab, _rows(j, SSD_HIST, q), :] = xbc_ref[:, cols]
    for t in range(XBC_TILES):
        slab, j, cols = _tile(t)
        acc = cbias_ref[:, cols] + cw_ref[0:1, cols] * ext_scr[slab, _rows(j, 0, q), :]
        for k in range(1, SSD_CONV_W):
            acc = acc + cw_ref[k:k + 1, cols] * ext_scr[slab, _rows(j, k, q), :]
        xc_scr[:, cols] = _silu(acc)
    for t in range(XBC_TILES):
        slab, j, cols = _tile(t)
        ext_scr[slab, _rows(j, 0, SUBLANES), :] = ext_scr[slab, _rows(j, q, SUBLANES), :]

    for g in range(GROUPS):
        gcols = slice(g * GROUP_W, (g + 1) * GROUP_W)
        bcols = slice(D_SSM + g * STATE, D_SSM + (g + 1) * STATE)
        ccols = slice(D_SSM + (GROUPS + g) * STATE, D_SSM + (GROUPS + g + 1) * STATE)
        bg16 = xc_scr[:, bcols].astype(BF16)
        cg16 = xc_scr[:, ccols].astype(BF16)
        cb = _nt_dot(cg16, bg16)
        hg = hout_ref[gcols, :]
        y_off = _nt_dot(cg16, hg.astype(BF16))
        for k in range(HEADS_PER_GROUP // 2):
            h1 = g * HEADS_PER_GROUP + 2 * k
            h2 = h1 + 1
            pr = h1 * HEAD_DIM
            pcols = slice(pr, pr + LANES)
            pair = xc_scr[:, pcols]
            y_scr[:, pcols] = (_pair_y_diag(cb, cum2, cum2_t, dt_t, causal, h1, pair)
                               + y_off[:, k * LANES:(k + 1) * LANES] * ecx_scr[:, pcols]
                               + dexp_ref[:, pcols] * pair)
            pair_t = pair.T
            xw = jnp.concatenate([pair_t[:half] * wend_t[h1:h1 + 1, :],
                                  pair_t[half:] * wend_t[h2:h2 + 1, :]], axis=0).astype(BF16)
            s_new = jnp.dot(xw, bg16, preferred_element_type=F32)
            hp = hg[k * LANES:(k + 1) * LANES]
            hout_ref[pcols, :] = jnp.concatenate([hp[:half] * cdec[h1:h1 + 1, :],
                                                  hp[half:] * cdec[h2:h2 + 1, :]], axis=0) + s_new

    _gate_norm_store(y_scr, z_ref, nw_ref, y_ref)

    @pl.when(c == n_chunks - 1)
    def _conv_state():
        cbout_ref[...] = xbc_ref[q - SSD_HIST:q, :]


def _ssd_prompt(proj, dtraw, h0, cb8, layer, cw, cbias, dtb, alog, dexp, nw, expand, *, batch, seq):
    q = SSD_CHUNK
    nc = seq // q
    xbc_blk = (N_MAIN - D_XBC) // D_XBC
    rowmap = lambda b, c: (b * nc + c, 0)
    per_layer = lambda b, c: (layer, 0, 0)
    per_batch = lambda b, c: (b, 0, 0)
    return pl.pallas_call(
        functools.partial(_ssd_prompt_kernel, n_chunks=nc),
        grid=(batch, nc),
        in_specs=[
            pl.BlockSpec((q, D_SSM), rowmap),
            pl.BlockSpec((q, D_XBC), lambda b, c: (b * nc + c, xbc_blk)),
            pl.BlockSpec((q, LANES), rowmap),
            pl.BlockSpec((None, D_SSM, STATE), per_batch),
            pl.BlockSpec((None, SUBLANES, D_XBC), per_batch),
            pl.BlockSpec((None, SSD_CONV_W, D_XBC), per_layer),
            pl.BlockSpec((None, 1, D_XBC), per_layer),
            pl.BlockSpec((None, 1, LANES), per_layer),
            pl.BlockSpec((None, 1, LANES), per_layer),
            pl.BlockSpec((None, 1, D_SSM), per_layer),
            pl.BlockSpec((None, 1, D_SSM), per_layer),
            pl.BlockSpec((3 * LANES, D_SSM), lambda b, c: (0, 0)),
        ],
        out_specs=[
            pl.BlockSpec((q, D_SSM), rowmap),
            pl.BlockSpec((None, D_SSM, STATE), per_batch),
            pl.BlockSpec((None, SSD_HIST, D_XBC), per_batch),
        ],
        out_shape=[
            jax.ShapeDtypeStruct((proj.shape[0], D_SSM), BF16),
            jax.ShapeDtypeStruct((batch, D_SSM, STATE), F32),
            jax.ShapeDtypeStruct((batch, SSD_HIST, D_XBC), F32),
        ],
        scratch_shapes=[
            pltpu.VMEM((XBC_SLABS, TILE_GROUP * (q + SUBLANES), LANES), F32),
            pltpu.VMEM((q, D_XBC), F32),
            pltpu.VMEM((q, D_SSM), F32),
            pltpu.VMEM((q, D_SSM), F32),
        ],
        compiler_params=pltpu.CompilerParams(
            dimension_semantics=("parallel", "arbitrary"), vmem_limit_bytes=VMEM_LIMIT),
        name="ssd_prompt",
    )(proj, proj, dtraw, h0, cb8, cw, cbias, dtb, alog, dexp, nw, expand)


N_SSD_SAMPLE_IN = 12


def _ssd_sample_kernel(*refs, n_sub, n_alias):
    (z_ref, xbc_ref, dt_ref, h0_ref, cb_ref, cw_ref, cbias_ref, dtb_ref, alog_ref,
     dexp_ref, nw_ref, e_ref) = refs[:N_SSD_SAMPLE_IN]
    (y_ref, hout_ref, cbout_ref,
     ext_scr, xc_scr, y_scr, xwt_scr, ecx_scr, cd_scr, cum_scr) = refs[N_SSD_SAMPLE_IN + n_alias:]
    q = SSD_CHUNK
    tb, sb, ls = TILE_BATCH, STATE_BATCH, DEC_SEQ
    sub = pl.program_id(1)
    row = lax.broadcasted_iota(jnp.int32, (q, q), 0)
    col = lax.broadcasted_iota(jnp.int32, (q, q), 1)
    same = (row // ls) == (col // ls)
    mask = jnp.logical_and(same, row >= col)

    @pl.when(sub == 0)
    def _token_space():
        cbout_ref[...] = xbc_ref[...].reshape(tb, ls, D_XBC)[:, ls - SSD_HIST:ls, :]
        pad = jnp.zeros((SUBLANES, LANES), F32)
        tok = lax.broadcasted_iota(jnp.int32, (q, LANES), 0) % ls
        for t in range(XBC_TILES):
            slab, j, cols = _tile(t)
            ext_scr[0, slab, _rows(j, 0, SUBLANES), :] = pad
            ext_scr[0, slab, _rows(j, SSD_HIST, q), :] = xbc_ref[:, cols]
            ext_scr[1, slab, _rows(j, 0, q), :] = cb_ref[:, :, cols].reshape(q, LANES)
            ext_scr[1, slab, _rows(j, q, SUBLANES), :] = pad
        for t in range(XBC_TILES):
            slab, j, cols = _tile(t)
            acc = cbias_ref[:, cols] + cw_ref[SSD_HIST:SSD_CONV_W, cols] * xbc_ref[:, cols]
            for s in range(1, SSD_CONV_W):
                k = SSD_HIST - s
                win = jnp.where(tok >= s, ext_scr[0, slab, _rows(j, k, q), :],
                                ext_scr[1, slab, _rows(j, k, q), :])
                acc = acc + cw_ref[k:k + 1, cols] * win
            xc_scr[:, cols] = _silu(acc)

        dt, a = _dt_terms(dt_ref, dtb_ref, alog_ref)
        cum = jnp.dot(mask.astype(F32), a, precision=HIGHEST, preferred_element_type=F32)
        tot = jnp.dot(same.astype(F32), a, precision=HIGHEST, preferred_element_type=F32)
        cum_t = cum.T
        dt_t = dt.T
        cum_scr[0] = cum * LOG2E
        cum_scr[1] = cum_t * LOG2E
        cum_scr[2] = dt_t
        tot_t = tot.T
        wend_t = jnp.exp(tot_t - cum_t) * dt_t
        cdec_t = jnp.exp(tot_t)
        for b in range(tb):
            cd_scr[b] = jnp.broadcast_to(cdec_t[:, b * ls:b * ls + 1], (LANES, STATE))
        ecx_scr[...] = _expand_heads(jnp.exp(cum), e_ref)

        for p in range(HEADS // 2):
            pr = p * LANES
            pair = xc_scr[:, pr:pr + LANES]
            y_scr[:, pr:pr + LANES] = dexp_ref[:, pr:pr + LANES] * pair
            pair_t = pair.T
            for hh in range(2):
                h = 2 * p + hh
                xwt_scr[h * HEAD_DIM:(h + 1) * HEAD_DIM, :] = (
                    pair_t[hh * HEAD_DIM:(hh + 1) * HEAD_DIM, :] * wend_t[h:h + 1, :]).astype(BF16)

    for g in range(GROUPS):
        @pl.when(sub == g)
        def _group_y_diag(g=g):
            bcols = slice(D_SSM + g * STATE, D_SSM + (g + 1) * STATE)
            ccols = slice(D_SSM + (GROUPS + g) * STATE, D_SSM + (GROUPS + g + 1) * STATE)
            cb = _nt_dot(xc_scr[:, ccols].astype(BF16), xc_scr[:, bcols].astype(BF16))
            cum2, cum2_t, dt_t = cum_scr[0], cum_scr[1], cum_scr[2]
            for k in range(HEADS_PER_GROUP // 2):
                h1 = g * HEADS_PER_GROUP + 2 * k
                pr = h1 * HEAD_DIM
                pair = xc_scr[:, pr:pr + LANES]
                y_scr[:, pr:pr + LANES] = (y_scr[:, pr:pr + LANES]
                                           + _pair_y_diag(cb, cum2, cum2_t, dt_t, mask, h1, pair))

    nrow = sb * ls
    rows = pl.ds(pl.multiple_of(sub * nrow, nrow), nrow)
    local_seq = lax.broadcasted_iota(jnp.int32, (nrow, STATE), 0) // ls
    tile_seq = lax.broadcasted_iota(jnp.int32, (q, STATE), 0) // ls
    for g in range(GROUPS):
        gcols = slice(g * GROUP_W, (g + 1) * GROUP_W)
        bcols = slice(D_SSM + g * STATE, D_SSM + (g + 1) * STATE)
        ccols = slice(D_SSM + (GROUPS + g) * STATE, D_SSM + (GROUPS + g + 1) * STATE)
        cg = xc_scr[rows, ccols]
        zero = jnp.zeros_like(cg)
        lhs = jnp.concatenate([jnp.where(local_seq == b, cg, zero) for b in range(sb)],
                              axis=1).astype(BF16)
        hcat = jnp.concatenate([h0_ref[b, gcols, :].astype(BF16) for b in range(sb)], axis=1)
        y_off = _nt_dot(lhs, hcat)
        y_scr[rows, gcols] = y_scr[rows, gcols] + y_off * ecx_scr[rows, gcols]

        bg = xc_scr[:, bcols]
        bzero = jnp.zeros_like(bg)
        bsel = jnp.concatenate([jnp.where(tile_seq == sub * sb + b, bg, bzero) for b in range(sb)],
                               axis=1).astype(BF16)
        s_new = jnp.dot(xwt_scr[gcols, :], bsel, preferred_element_type=F32)
        for b in range(sb):
            cdb = cd_scr[sub * sb + b]
            for hl in range(HEADS_PER_GROUP):
                h = g * HEADS_PER_GROUP + hl
                hrows = slice(h * HEAD_DIM, (h + 1) * HEAD_DIM)
                hout_ref[b, hrows, :] = (
                    h0_ref[b, hrows, :] * cdb[h:h + 1, :]
                    + s_new[hl * HEAD_DIM:(hl + 1) * HEAD_DIM, b * STATE:(b + 1) * STATE])

    @pl.when(sub == n_sub - 1)
    def _finish():
        _gate_norm_store(y_scr, z_ref, nw_ref, y_ref)


def _ssd_sample(proj, dtraw, h0, cb0, y_full, h_stack, layer, cw, cbias, dtb, alog, dexp, nw, expand,
                *, batch, row0):
    q = SSD_CHUNK
    tb, sb = TILE_BATCH, STATE_BATCH
    n_sub = tb // sb
    assert n_sub == GROUPS
    n_tiles = batch // tb
    rb0 = row0 // q
    xbc_blk = (N_MAIN - D_XBC) // D_XBC
    rowmap = lambda i, s: (rb0 + i, 0)
    per_layer = lambda i, s: (layer, 0, 0)
    aliased = [y_full] + ([] if h_stack is None else [h_stack])
    aliases = {N_SSD_SAMPLE_IN: 0}
    if h_stack is not None:
        aliases[N_SSD_SAMPLE_IN + 1] = 1
    return pl.pallas_call(
        functools.partial(_ssd_sample_kernel, n_sub=n_sub, n_alias=len(aliased)),
        grid=(n_tiles, n_sub),
        input_output_aliases=aliases,
        in_specs=[
            pl.BlockSpec((q, D_SSM), rowmap),
            pl.BlockSpec((q, D_XBC), lambda i, s: (rb0 + i, xbc_blk)),
            pl.BlockSpec((q, LANES), rowmap),
            pl.BlockSpec((None, sb, D_SSM, STATE), lambda i, s: (layer, i * n_sub + s, 0, 0)),
            pl.BlockSpec((None, tb, SUBLANES, D_XBC), lambda i, s: (layer, i, 0, 0)),
            pl.BlockSpec((None, SSD_CONV_W, D_XBC), per_layer),
            pl.BlockSpec((None, 1, D_XBC), per_layer),
            pl.BlockSpec((None, 1, LANES), per_layer),
            pl.BlockSpec((None, 1, LANES), per_layer),
            pl.BlockSpec((None, 1, D_SSM), per_layer),
            pl.BlockSpec((None, 1, D_SSM), per_layer),
            pl.BlockSpec((3 * LANES, D_SSM), lambda i, s: (0, 0)),
        ] + [pl.BlockSpec(memory_space=pl.ANY) for _ in aliased],
        out_specs=[
            pl.BlockSpec((q, D_SSM), rowmap),
            pl.BlockSpec((None, sb, D_SSM, STATE), lambda i, s: (layer, i * n_sub + s, 0, 0)),
            pl.BlockSpec((tb, SSD_HIST, D_XBC), lambda i, s: (i, 0, 0)),
        ],
        out_shape=[
            jax.ShapeDtypeStruct(y_full.shape, BF16),
            jax.ShapeDtypeStruct(h0.shape, F32),
            jax.ShapeDtypeStruct((batch, SSD_HIST, D_XBC), F32),
        ],
        scratch_shapes=[
            pltpu.VMEM((2, XBC_SLABS, TILE_GROUP * (q + SUBLANES), LANES), F32),
            pltpu.VMEM((q, D_XBC), F32),
            pltpu.VMEM((q, D_SSM), F32),
            pltpu.VMEM((D_SSM, q), BF16),
            pltpu.VMEM((q, D_SSM), F32),
            pltpu.VMEM((tb, LANES, STATE), F32),
            pltpu.VMEM((3, q, LANES), F32),
        ],
        compiler_params=pltpu.CompilerParams(
            dimension_semantics=("parallel", "arbitrary"), vmem_limit_bytes=VMEM_LIMIT),
        name="ssd_sample",
    )(proj, proj, dtraw, h0, cb0, cw, cbias, dtb, alog, dexp, nw, expand, *aliased)


def _layernorm_silu_store(vc_tiles, lnw_ref, lnb_ref, y_ref):
    total = vc_tiles[0]
    for v in vc_tiles[1:]:
        total = total + v
    mu = total.sum(axis=-1, keepdims=True) * (1.0 / D_CONV)
    sq = None
    for v in vc_tiles:
        d = v - mu
        sq = d * d if sq is None else sq + d * d
    rstd = lax.rsqrt(sq.sum(axis=-1, keepdims=True) * (1.0 / D_CONV) + EPS)
    for t, v in enumerate(vc_tiles):
        cols = slice(t * LANES, (t + 1) * LANES)
        o = ((v - mu) * rstd) * lnw_ref[:, cols] + lnb_ref[:, cols]
        y_ref[:, cols] = _silu(o).astype(y_ref.dtype)


def _conf_prompt_kernel(a_ref, b_ref, buf_ref, w_ref, bias_ref, lnw_ref, lnb_ref, y_ref, bufout_ref,
                        ext_scr, vc_scr, *, n_steps):
    r = CONF_ROWS
    c = pl.program_id(1)
    acc_rows = SSD_CHUNK

    @pl.when(c == 0)
    def _init():
        for t in range(CONV_TILES):
            slab, j, cols = _tile(t)
            ext_scr[slab, _rows(j, 0, CONF_PAD), :] = buf_ref[:, cols]

    for t in range(CONV_TILES):
        slab, j, cols = _tile(t)
        ext_scr[slab, _rows(j, CONF_HIST, r), :] = a_ref[:, cols] * _sigmoid(b_ref[:, cols])

    def slab_body(slab, carry):
        for j in range(TILE_GROUP):
            for rb in range(r // acc_rows):
                base = rb * acc_rows
                acc = bias_ref[slab, j:j + 1, :] + w_ref[slab, 0, j:j + 1, :] * ext_scr[slab, _rows(j, base, acc_rows), :]
                for k in range(1, CONF_W):
                    acc = acc + w_ref[slab, k, j:j + 1, :] * ext_scr[slab, _rows(j, base + k, acc_rows), :]
                vc_scr[slab, j, base:base + acc_rows, :] = acc
        return carry

    lax.fori_loop(0, CONV_SLABS, slab_body, 0)

    @pl.when(c == n_steps - 1)
    def _state():
        for t in range(CONV_TILES):
            slab, j, cols = _tile(t)
            bufout_ref[:, cols] = ext_scr[slab, _rows(j, r, CONF_PAD), :][:CONF_HIST]

    for t in range(CONV_TILES):
        slab, j, cols = _tile(t)
        ext_scr[slab, _rows(j, 0, CONF_PAD), :] = ext_scr[slab, _rows(j, r, CONF_PAD), :]

    _layernorm_silu_store([vc_scr[t // TILE_GROUP, t % TILE_GROUP] for t in range(CONV_TILES)],
                          lnw_ref, lnb_ref, y_ref)


def _conf_prompt(proj, buf32, layer, w_tiles, bias_tiles, lnw, lnb, *, batch, seq):
    r = CONF_ROWS
    ns = seq // r
    per_layer = lambda b, c: (layer, 0, 0)
    return pl.pallas_call(
        functools.partial(_conf_prompt_kernel, n_steps=ns),
        grid=(batch, ns),
        in_specs=[
            pl.BlockSpec((r, D_CONV), lambda b, c: (b * ns + c, 1)),
            pl.BlockSpec((r, D_CONV), lambda b, c: (b * ns + c, 2)),
            pl.BlockSpec((None, CONF_PAD, D_CONV), lambda b, c: (b, 0, 0)),
            pl.BlockSpec((None, CONV_SLABS, CONF_W, TILE_GROUP, LANES), lambda b, c: (layer, 0, 0, 0, 0)),
            pl.BlockSpec((None, CONV_SLABS, TILE_GROUP, LANES), lambda b, c: (layer, 0, 0, 0)),
            pl.BlockSpec((None, 1, D_CONV), per_layer),
            pl.BlockSpec((None, 1, D_CONV), per_layer),
        ],
        out_specs=[
            pl.BlockSpec((r, D_CONV), lambda b, c: (b * ns + c, 0)),
            pl.BlockSpec((None, CONF_HIST, D_CONV), lambda b, c: (b, 0, 0)),
        ],
        out_shape=[
            jax.ShapeDtypeStruct((proj.shape[0], D_CONV), BF16),
            jax.ShapeDtypeStruct((batch, CONF_HIST, D_CONV), F32),
        ],
        scratch_shapes=[
            pltpu.VMEM((CONV_SLABS, TILE_GROUP * (r + CONF_PAD), LANES), F32),
            pltpu.VMEM((CONV_SLABS, TILE_GROUP, r, LANES), F32),
        ],
        compiler_params=pltpu.CompilerParams(
            dimension_semantics=("parallel", "arbitrary"), vmem_limit_bytes=VMEM_LIMIT),
        name="conf_prompt",
    )(proj, proj, buf32, w_tiles, bias_tiles, lnw, lnb)


N_CONF_SAMPLE_IN = 7
CONF_SEQ_ROWS = CONF_HIST + DEC_SEQ + 2


def _conf_sample_kernel(*refs, n_alias):
    a_ref, b_ref, buf_ref, w_ref, bias_ref, lnw_ref, lnb_ref = refs[:N_CONF_SAMPLE_IN]
    y_ref, bufout_ref, ext_scr, vc_scr = refs[N_CONF_SAMPLE_IN + n_alias:]
    tb, ls = TILE_BATCH, DEC_SEQ
    full_rows = (CONF_HIST // SUBLANES) * SUBLANES
    rest = CONF_HIST - full_rows

    for t in range(CONV_TILES):
        slab, j, cols = _tile(t)
        v = a_ref[:, cols] * _sigmoid(b_ref[:, cols])
        for b in range(tb):
            ext_scr[slab, b, _rows(j, 0, full_rows), :] = buf_ref[b, 0:full_rows, cols]
            ext_scr[slab, b, _rows(j, full_rows, rest), :] = buf_ref[b, full_rows:CONF_HIST, cols]
            ext_scr[slab, b, _rows(j, CONF_HIST, ls), :] = v[b * ls:(b + 1) * ls, :]

    def slab_body(slab, carry):
        for j in range(TILE_GROUP):
            accs = [bias_ref[slab, j:j + 1, :] + w_ref[slab, 0, j:j + 1, :] * ext_scr[slab, b, _rows(j, 0, ls), :]
                    for b in range(tb)]
            for k in range(1, CONF_W):
                wk = w_ref[slab, k, j:j + 1, :]
                accs = [acc + wk * ext_scr[slab, b, _rows(j, k, ls), :] for b, acc in enumerate(accs)]
            for b in range(tb):
                vc_scr[slab, j, b * ls:(b + 1) * ls, :] = accs[b]
        return carry

    lax.fori_loop(0, CONV_SLABS, slab_body, 0)

    for t in range(CONV_TILES):
        slab, j, cols = _tile(t)
        for b in range(tb):
            bufout_ref[b, 0:full_rows, cols] = ext_scr[slab, b, _rows(j, ls, full_rows), :]
            bufout_ref[b, full_rows:CONF_HIST, cols] = ext_scr[slab, b, _rows(j, ls + full_rows, rest), :]

    _layernorm_silu_store([vc_scr[t // TILE_GROUP, t % TILE_GROUP] for t in range(CONV_TILES)],
                          lnw_ref, lnb_ref, y_ref)


def _conf_sample(proj, buf, y_full, buf_stack, layer, w_tiles, bias_tiles, lnw, lnb, *, batch, row0):
    tb, ls = TILE_BATCH, DEC_SEQ
    rows = tb * ls
    rb0 = row0 // rows
    per_layer = lambda i: (layer, 0, 0)
    aliased = [y_full] + ([] if buf_stack is None else [buf_stack])
    aliases = {N_CONF_SAMPLE_IN: 0}
    if buf_stack is not None:
        aliases[N_CONF_SAMPLE_IN + 1] = 1
    return pl.pallas_call(
        functools.partial(_conf_sample_kernel, n_alias=len(aliased)),
        grid=(batch // tb,),
        input_output_aliases=aliases,
        in_specs=[
            pl.BlockSpec((rows, D_CONV), lambda i: (rb0 + i, 1)),
            pl.BlockSpec((rows, D_CONV), lambda i: (rb0 + i, 2)),
            pl.BlockSpec((None, tb, CONF_HIST, D_CONV), lambda i: (layer, i, 0, 0)),
            pl.BlockSpec((None, CONV_SLABS, CONF_W, TILE_GROUP, LANES), lambda i: (layer, 0, 0, 0, 0)),
            pl.BlockSpec((None, CONV_SLABS, TILE_GROUP, LANES), lambda i: (layer, 0, 0, 0)),
            pl.BlockSpec((None, 1, D_CONV), per_layer),
            pl.BlockSpec((None, 1, D_CONV), per_layer),
        ] + [pl.BlockSpec(memory_space=pl.ANY) for _ in aliased],
        out_specs=[
            pl.BlockSpec((rows, D_CONV), lambda i: (rb0 + i, 0)),
            pl.BlockSpec((None, tb, CONF_HIST, D_CONV), lambda i: (layer, i, 0, 0)),
        ],
        out_shape=[
            jax.ShapeDtypeStruct(y_full.shape, BF16),
            jax.ShapeDtypeStruct(buf.shape, F32),
        ],
        scratch_shapes=[
            pltpu.VMEM((CONV_SLABS, tb, TILE_GROUP * CONF_SEQ_ROWS, LANES), F32),
            pltpu.VMEM((CONV_SLABS, TILE_GROUP, rows, LANES), F32),
        ],
        compiler_params=pltpu.CompilerParams(
            dimension_semantics=("parallel",), vmem_limit_bytes=VMEM_LIMIT),
        name="conf_sample",
    )(proj, proj, buf, w_tiles, bias_tiles, lnw, lnb, *aliased)


def kernel(x_prompt, x_sample, state_ssm, state_ssd_conv, state_conformer_conv, w_in, ssd_conv_w, ssd_conv_b,
           dt_bias, a_log, d_skip, ssd_norm_w, conf_conv_w, conf_conv_b, conf_norm_w, conf_norm_b, w_out,
           norm_pre_mix, norm_post_mix, norm_pre_mlp, norm_post_mlp, w_up, w_down):
    pb, pl_len, d = x_prompt.shape
    sbatch, s_len, _ = x_sample.shape
    tp = pb * pl_len
    ts = sbatch * s_len
    depth = w_in.shape[0]

    w_main = jnp.concatenate([w_in[..., :Z_END], w_in[..., DT_END:], w_in[..., Z_END:XBC_END]],
                             axis=-1).astype(BF16)
    w_dt = jnp.pad(w_in[..., XBC_END:DT_END], ((0, 0), (0, 0), (0, LANES - HEADS))).astype(BF16)
    w_out16 = w_out.astype(BF16)
    w_up16 = w_up.astype(BF16)
    w_down16 = w_down.astype(BF16)

    vec = lambda a: a.reshape(depth, 1, a.shape[-1])
    pad_heads = lambda a: jnp.pad(a, ((0, 0), (0, LANES - HEADS))).reshape(depth, 1, LANES)
    dtb = pad_heads(dt_bias)
    alog = pad_heads(a_log)
    dexp = vec(jnp.repeat(d_skip, HEAD_DIM, axis=-1))
    cbias = vec(ssd_conv_b)
    ssd_nw = vec(ssd_norm_w)
    conf_w_tiles = conf_conv_w.reshape(depth, CONF_W, CONV_SLABS, TILE_GROUP, LANES).transpose(0, 2, 1, 3, 4)
    conf_b_tiles = conf_conv_b.reshape(depth, CONV_SLABS, TILE_GROUP, LANES)
    lnw = vec(conf_norm_w)
    lnb = vec(conf_norm_b)
    n_pre_mix, n_post_mix = vec(norm_pre_mix), vec(norm_post_mix)
    n_pre_mlp, n_post_mlp = vec(norm_pre_mlp), vec(norm_post_mlp)
    expand = (lax.broadcasted_iota(jnp.int32, (LANES, D_SSM), 1) // HEAD_DIM
              == lax.broadcasted_iota(jnp.int32, (LANES, D_SSM), 0)).astype(BF16)
    expand = jnp.tile(expand, (3, 1))

    h0_sample = state_ssm.reshape(depth, sbatch, D_SSM, STATE)
    cb0_sample = jnp.pad(state_ssd_conv, ((0, 0), (0, 0), (0, SUBLANES - SSD_HIST), (0, 0)))
    h0_prompt = jnp.zeros((pb, D_SSM, STATE), F32)
    cb0_prompt = jnp.zeros((pb, SUBLANES, D_XBC), F32)
    conf0_prompt = jnp.zeros((pb, CONF_PAD, D_CONV), F32)

    x = jnp.concatenate([x_prompt.reshape(tp, d), x_sample.reshape(ts, d)], axis=0)
    u, dtraw = _norm_call(x, None, None, 0, n_pre_mix, w_dt, 0)

    p_ssm, p_sc, p_cc, s_sc = [], [], [], []
    s_ssm = s_cc = None
    for i in range(depth):
        proj = _matmul([u], w_main, i, tm=1024, tn=1024, name="in_proj")
        ssd_y, h_p, sc_p = _ssd_prompt(proj, dtraw, h0_prompt, cb0_prompt, i, ssd_conv_w, cbias, dtb, alog,
                                       dexp, ssd_nw, expand, batch=pb, seq=pl_len)
        ssd_y, s_ssm, sc_s = _ssd_sample(proj, dtraw, h0_sample, cb0_sample, ssd_y, s_ssm, i, ssd_conv_w,
                                         cbias, dtb, alog, dexp, ssd_nw, expand, batch=sbatch, row0=tp)
        conf_y, cc_p = _conf_prompt(proj, conf0_prompt, i, conf_w_tiles, conf_b_tiles, lnw, lnb,
                                    batch=pb, seq=pl_len)
        conf_y, s_cc = _conf_sample(proj, state_conformer_conv, conf_y, s_cc, i, conf_w_tiles, conf_b_tiles,
                                    lnw, lnb, batch=sbatch, row0=tp)
        mix = _matmul([ssd_y, conf_y], w_out16, i, tm=1024, tn=512, name="out_proj")
        x, u2 = _norm_call(x, mix, n_post_mix, i, n_pre_mlp, None, i)
        hid = _matmul([u2], w_up16, i, tm=1024, tn=1024, relu2=True, out_dtype=BF16, name="up_proj")
        down = _matmul([hid], w_down16, i, tm=512, tn=512, name="down_proj")
        if i + 1 < depth:
            x, u, dtraw = _norm_call(x, down, n_post_mlp, i, n_pre_mix, w_dt, i + 1)
        else:
            (x,) = _norm_call(x, down, n_post_mlp, i, None, None, i)
        p_ssm.append(h_p)
        p_sc.append(sc_p)
        p_cc.append(cc_p)
        s_sc.append(sc_s)

    y_prompt = x[:tp].reshape(pb, pl_len, d)
    y_sample = x[tp:].reshape(sbatch, s_len, d)
    state_shape = lambda b: (depth, b, HEADS, HEAD_DIM, STATE)
    return (y_prompt, y_sample,
            jnp.stack(p_ssm).reshape(state_shape(pb)), jnp.stack(p_sc), jnp.stack(p_cc),
            s_ssm.reshape(state_shape(sbatch)), jnp.stack(s_sc), s_cc)
```

```python
import functools

import jax
import jax.numpy as jnp
from jax import lax
from jax.experimental import pallas as pl
from jax.experimental.pallas import tpu as pltpu

F32 = jnp.float32
BF16 = jnp.bfloat16
HIGHEST = lax.Precision.HIGHEST

D_MODEL = 2048
DEPTH = 4
D_SSM = 2048
D_CONV = 2048
HEAD_DIM = 64
HEADS = 32
GROUPS = 4
HEADS_PER_GROUP = HEADS // GROUPS
GROUP_W = D_SSM // GROUPS
STATE = 128
SSD_CONV_W = 4
SSD_HIST = SSD_CONV_W - 1
D_XBC = D_SSM + 2 * GROUPS * STATE
CONF_W = 31
CONF_HIST = CONF_W - 1
D_FF = 4 * D_MODEL
EPS = 1e-6

LANES = 128
SUBLANES = 8
TILE_GROUP = 4

N_MAIN = D_SSM + 2 * D_CONV + D_XBC
Z_END = D_SSM
XBC_END = Z_END + D_XBC
DT_END = XBC_END + HEADS

SSD_CHUNK = 128
DEC_SEQ = 8
TILE_BATCH = SSD_CHUNK // DEC_SEQ
STATE_BATCH = 4
CONF_ROWS = 256
CONF_PAD = 32
CONV_TILES = D_CONV // LANES
XBC_TILES = D_XBC // LANES
CONV_SLABS = CONV_TILES // TILE_GROUP
XBC_SLABS = XBC_TILES // TILE_GROUP

VMEM_LIMIT = 56 * 1024 * 1024


def _sigmoid(x):
    return 0.5 + 0.5 * jnp.tanh(0.5 * x)


def _silu(x):
    h = 0.5 * x
    return h + h * jnp.tanh(h)


def _softplus(x):
    return jnp.maximum(x, 0.0) + jnp.log1p(jnp.exp(-jnp.abs(x)))


def _rms_scale(x):
    return x * lax.rsqrt(jnp.mean(x * x, axis=-1, keepdims=True) + EPS)


def _tile(t):
    return t // TILE_GROUP, t % TILE_GROUP, slice(t * LANES, (t + 1) * LANES)


def _rows(j, start, n):
    return pl.ds(TILE_GROUP * start + j, n, stride=TILE_GROUP)


def _entry_kernel(xp_ref, xs_ref, wnext_ref, wdt_ref, xo_ref, u_ref, dt_ref, *, prompt_blocks):
    def emit(x):
        xo_ref[...] = x
        u = (_rms_scale(x) * wnext_ref[...]).astype(BF16)
        u_ref[...] = u
        dt_ref[...] = jnp.dot(u, wdt_ref[...], preferred_element_type=F32)

    i = pl.program_id(0)

    @pl.when(i < prompt_blocks)
    def _prompt():
        emit(xp_ref[...])

    @pl.when(i >= prompt_blocks)
    def _sample():
        emit(xs_ref[...])


def _entry_call(xp, xs, wnext, wdt, *, rows=256):
    tp, d = xp.shape
    ts = xs.shape[0]
    t = tp + ts
    pblocks = tp // rows
    row_spec = pl.BlockSpec((rows, d), lambda i: (i, 0))
    return pl.pallas_call(
        functools.partial(_entry_kernel, prompt_blocks=pblocks),
        grid=(t // rows,),
        in_specs=[
            pl.BlockSpec((rows, d), lambda i: (jnp.minimum(i, pblocks - 1), 0)),
            pl.BlockSpec((rows, d), lambda i: (jnp.maximum(i - pblocks, 0), 0)),
            pl.BlockSpec((None, 1, d), lambda i: (0, 0, 0)),
            pl.BlockSpec((None, d, LANES), lambda i: (0, 0, 0)),
        ],
        out_specs=[row_spec, row_spec, pl.BlockSpec((rows, LANES), lambda i: (i, 0))],
        out_shape=[jax.ShapeDtypeStruct((t, d), F32), jax.ShapeDtypeStruct((t, d), BF16),
                   jax.ShapeDtypeStruct((t, LANES), F32)],
        compiler_params=pltpu.CompilerParams(
            dimension_semantics=("arbitrary",), vmem_limit_bytes=VMEM_LIMIT),
        name="entry_norm",
    )(xp, xs, wnext, wdt)


def _mm_kernel(*refs, cast, has_into, relu2):
    x_ref, w_ref = refs[:2]
    o_ref = refs[2 + int(has_into)]
    if cast:
        wbf_scr = refs[-1]

        @pl.when(pl.program_id(1) == 0)
        def _round_weights():
            wbf_scr[...] = w_ref[...].astype(BF16)

        w = wbf_scr[...]
    else:
        w = w_ref[...]
    acc = jnp.dot(x_ref[...], w, preferred_element_type=F32)
    if relu2:
        acc = jnp.maximum(acc, 0.0)
        acc = acc * acc
    o_ref[...] = acc.astype(o_ref.dtype)


def _matmul(x, w, layer, *, tm, tn, n_blocks, w_col=lambda j: j, out_col=lambda j: j, out_cols=None,
            into=None, relu2=False, out_dtype=F32, name="matmul"):
    m, k = x.shape
    cast = w.dtype != BF16
    args = [x, w]
    in_specs = [pl.BlockSpec((tm, k), lambda j, i: (i, 0)),
                pl.BlockSpec((None, k, tn), lambda j, i: (layer, 0, w_col(j)))]
    aliases = {}
    if into is not None:
        aliases = {len(args): 0}
        args.append(into)
        in_specs.append(pl.BlockSpec(memory_space=pl.ANY))
        out_shape = jax.ShapeDtypeStruct(into.shape, into.dtype)
    else:
        out_shape = jax.ShapeDtypeStruct((m, out_cols or n_blocks * tn), out_dtype)
    return pl.pallas_call(
        functools.partial(_mm_kernel, cast=cast, has_into=into is not None, relu2=relu2),
        grid=(n_blocks, m // tm),
        in_specs=in_specs,
        out_specs=pl.BlockSpec((tm, tn), lambda j, i: (i, out_col(j))),
        out_shape=out_shape,
        input_output_aliases=aliases,
        scratch_shapes=[pltpu.VMEM((k, tn), BF16)] if cast else [],
        compiler_params=pltpu.CompilerParams(
            dimension_semantics=("parallel", "arbitrary"), vmem_limit_bytes=VMEM_LIMIT),
        name=name,
    )(*args)


def _proj_norm_kernel(*refs, part_steps, has_next, has_dt, split_block):
    n_parts = len(part_steps)
    x_refs = refs[:n_parts]
    w_ref, res_ref, wpost_ref = refs[n_parts:n_parts + 3]
    pos = n_parts + 3
    wnext_ref = wdt_ref = u_ref = dt_ref = None
    if has_next:
        wnext_ref = refs[pos]
        pos += 1
    if has_dt:
        wdt_ref = refs[pos]
        pos += 1
    n_xo = 1 if split_block is None else 2
    xo_refs = refs[pos:pos + n_xo]
    pos += n_xo
    if has_next:
        u_ref = refs[pos]
        pos += 1
    if has_dt:
        dt_ref = refs[pos]
        pos += 1
    acc_ref = refs[pos]

    i = pl.program_id(0)
    k = pl.program_id(1)
    n_k = sum(part_steps)
    start = 0
    for xr, steps in zip(x_refs, part_steps):
        @pl.when(jnp.logical_and(k >= start, k < start + steps))
        def _accumulate(xr=xr):
            d = jnp.dot(xr[...], w_ref[...], preferred_element_type=F32)

            @pl.when(k == 0)
            def _first():
                acc_ref[...] = d

            @pl.when(k != 0)
            def _rest():
                acc_ref[...] += d

        start += steps

    @pl.when(k == n_k - 1)
    def _epilogue():
        x = res_ref[...] + _rms_scale(acc_ref[...]) * wpost_ref[...]
        if split_block is None:
            xo_refs[0][...] = x
        else:
            @pl.when(i < split_block)
            def _head_rows():
                xo_refs[0][...] = x

            @pl.when(i >= split_block)
            def _tail_rows():
                xo_refs[1][...] = x
        if has_next:
            u = (_rms_scale(x) * wnext_ref[...]).astype(BF16)
            u_ref[...] = u
            if has_dt:
                dt_ref[...] = jnp.dot(u, wdt_ref[...], preferred_element_type=F32)


def _proj_norm(xs, w, layer, res, wpost, wnext, wdt, next_layer, *, split_rows=None, tm=512, tk=2048,
               name="proj_norm"):
    m, d = res.shape
    part_steps = tuple(x.shape[1] // tk for x in xs)
    n_k = sum(part_steps)
    has_next, has_dt = wnext is not None, wdt is not None
    in_specs, start = [], 0
    for steps in part_steps:
        in_specs.append(pl.BlockSpec(
            (tm, tk), functools.partial(lambda i, k, s, n: (i, jnp.clip(k - s, 0, n - 1)), s=start, n=steps)))
        start += steps
    row_spec = pl.BlockSpec((tm, d), lambda i, k: (i, 0))
    in_specs += [pl.BlockSpec((None, tk, d), lambda i, k: (layer, k, 0)), row_spec,
                 pl.BlockSpec((None, 1, d), lambda i, k: (layer, 0, 0))]
    args = [*xs, w, res, wpost]
    if has_next:
        in_specs.append(pl.BlockSpec((None, 1, d), lambda i, k: (next_layer, 0, 0)))
        args.append(wnext)
    if has_dt:
        in_specs.append(pl.BlockSpec((None, d, LANES), lambda i, k: (next_layer, 0, 0)))
        args.append(wdt)
    if split_rows is None:
        split_block = None
        out_specs = [row_spec]
        out_shape = [jax.ShapeDtypeStruct((m, d), F32)]
    else:
        split_block = split_rows // tm
        out_specs = [pl.BlockSpec((tm, d), lambda i, k: (jnp.minimum(i, split_block - 1), 0)),
                     pl.BlockSpec((tm, d), lambda i, k: (jnp.maximum(i - split_block, 0), 0))]
        out_shape = [jax.ShapeDtypeStruct((split_rows, d), F32),
                     jax.ShapeDtypeStruct((m - split_rows, d), F32)]
    if has_next:
        out_specs.append(row_spec)
        out_shape.append(jax.ShapeDtypeStruct((m, d), BF16))
    if has_dt:
        out_specs.append(pl.BlockSpec((tm, LANES), lambda i, k: (i, 0)))
        out_shape.append(jax.ShapeDtypeStruct((m, LANES), F32))
    return pl.pallas_call(
        functools.partial(_proj_norm_kernel, part_steps=part_steps, has_next=has_next, has_dt=has_dt,
                          split_block=split_block),
        grid=(m // tm, n_k),
        in_specs=in_specs,
        out_specs=out_specs,
        out_shape=out_shape,
        scratch_shapes=[pltpu.VMEM((tm, d), F32)],
        compiler_params=pltpu.CompilerParams(
            dimension_semantics=("arbitrary", "arbitrary"), vmem_limit_bytes=VMEM_LIMIT),
        name=name,
    )(*args)


def _nt_dot(a, b):
    return lax.dot_general(a, b, (((1,), (1,)), ((), ())), preferred_element_type=F32)


LOG2E = 1.4426950408889634


def _head_decay_weights(cb, cum2, cum2_t, dt_t, mask, h):
    seg2 = cum2[:, h:h + 1] - cum2_t[h:h + 1, :]
    decay = jnp.exp2(jnp.where(mask, seg2, -jnp.inf)) * dt_t[h:h + 1, :]
    return (cb * decay).astype(BF16)


def _pair_y_diag(cb, cum2, cum2_t, dt_t, mask, h1, pair):
    q = pair.shape[0]
    w_stack = jnp.concatenate([_head_decay_weights(cb, cum2, cum2_t, dt_t, mask, h1),
                               _head_decay_weights(cb, cum2, cum2_t, dt_t, mask, h1 + 1)], axis=0)
    yy = jnp.dot(w_stack, pair.astype(BF16), preferred_element_type=F32)
    lo = lax.broadcasted_iota(jnp.int32, pair.shape, 1) < HEAD_DIM
    return jnp.where(lo, yy[:q], yy[q:])


def _gate_norm_store(y_scr, z_ref, nw_ref, y_ref):
    for g in range(GROUPS):
        cols = slice(g * GROUP_W, (g + 1) * GROUP_W)
        gated = y_scr[:, cols] * _silu(z_ref[:, cols])
        y_ref[:, cols] = (_rms_scale(gated) * nw_ref[:, cols]).astype(y_ref.dtype)


def _expand_heads(v, e3_ref):
    hi = v.astype(BF16)
    r1 = v - hi.astype(F32)
    mid = r1.astype(BF16)
    lo = (r1 - mid.astype(F32)).astype(BF16)
    return jnp.dot(jnp.concatenate([hi, mid, lo], axis=1), e3_ref[...], preferred_element_type=F32)


def _dt_terms(dt_ref, dtb_ref, alog_ref):
    dt = _softplus(dt_ref[...] + dtb_ref[...])
    a = dt * (-jnp.exp(alog_ref[...]))
    return dt, a


def _ssd_prompt_kernel(z_ref, xbc_ref, dt_ref, h0_ref, cb_ref, cw_ref, cbias_ref, dtb_ref, alog_ref,
                       dexp_ref, nw_ref, e_ref, y_ref, hout_ref, cbout_ref,
                       ext_scr, xc_scr, y_scr, ecx_scr, *, n_chunks):
    q = SSD_CHUNK
    half = HEAD_DIM
    c = pl.program_id(1)

    @pl.when(c == 0)
    def _init():
        hout_ref[...] = h0_ref[...]
        for t in range(XBC_TILES):
            slab, j, cols = _tile(t)
            ext_scr[slab, _rows(j, 0, SUBLANES), :] = cb_ref[:, cols]

    dt, a = _dt_terms(dt_ref, dtb_ref, alog_ref)
    row = lax.broadcasted_iota(jnp.int32, (q, q), 0)
    col = lax.broadcasted_iota(jnp.int32, (q, q), 1)
    causal = row >= col
    cum = jnp.dot(causal.astype(F32), a, precision=HIGHEST, preferred_element_type=F32)
    cum_t = cum.T
    dt_t = dt.T
    tot = jnp.broadcast_to(cum_t[:, q - 1:q], (LANES, q))
    wend_t = jnp.exp(tot - cum_t) * dt_t
    cdec = jnp.exp(tot)
    ecx_scr[...] = _expand_heads(jnp.exp(cum), e_ref)
    cum2 = cum * LOG2E
    cum2_t = cum_t * LOG2E

    for t in range(XBC_TILES):
        slab, j, cols = _tile(t)
        ext_scr[slab, _rows(j, SSD_HIST, q), :] = xbc_ref[:, cols]
    for t in range(XBC_TILES):
        slab, j, cols = _tile(t)
        acc = cbias_ref[:, cols] + cw_ref[0:1, cols] * ext_scr[slab, _rows(j, 0, q), :]
        for k in range(1, SSD_CONV_W):
            acc = acc + cw_ref[k:k + 1, cols] * ext_scr[slab, _rows(j, k, q), :]
        xc_scr[:, cols] = _silu(acc)
    for t in range(XBC_TILES):
        slab, j, cols = _tile(t)
        ext_scr[slab, _rows(j, 0, SUBLANES), :] = ext_scr[slab, _rows(j, q, SUBLANES), :]

    for g in range(GROUPS):
        gcols = slice(g * GROUP_W, (g + 1) * GROUP_W)
        bcols = slice(D_SSM + g * STATE, D_SSM + (g + 1) * STATE)
        ccols = slice(D_SSM + (GROUPS + g) * STATE, D_SSM + (GROUPS + g + 1) * STATE)
        bg16 = xc_scr[:, bcols].astype(BF16)
        cg16 = xc_scr[:, ccols].astype(BF16)
        cb = _nt_dot(cg16, bg16)
        hg = hout_ref[gcols, :]
        y_off = _nt_dot(cg16, hg.astype(BF16))
        for k in range(HEADS_PER_GROUP // 2):
            h1 = g * HEADS_PER_GROUP + 2 * k
            h2 = h1 + 1
            pr = h1 * HEAD_DIM
            pcols = slice(pr, pr + LANES)
            pair = xc_scr[:, pcols]
            y_scr[:, pcols] = (_pair_y_diag(cb, cum2, cum2_t, dt_t, causal, h1, pair)
                               + y_off[:, k * LANES:(k + 1) * LANES] * ecx_scr[:, pcols]
                               + dexp_ref[:, pcols] * pair)
            pair_t = pair.T
            xw = jnp.concatenate([pair_t[:half] * wend_t[h1:h1 + 1, :],
                                  pair_t[half:] * wend_t[h2:h2 + 1, :]], axis=0).astype(BF16)
            s_new = jnp.dot(xw, bg16, preferred_element_type=F32)
            hp = hg[k * LANES:(k + 1) * LANES]
            hout_ref[pcols, :] = jnp.concatenate([hp[:half] * cdec[h1:h1 + 1, :],
                                                  hp[half:] * cdec[h2:h2 + 1, :]], axis=0) + s_new

    _gate_norm_store(y_scr, z_ref, nw_ref, y_ref)

    @pl.when(c == n_chunks - 1)
    def _conv_state():
        cbout_ref[...] = xbc_ref[q - SSD_HIST:q, :]


def _ssd_prompt(proj, dtraw, h0, cb8, layer, cw, cbias, dtb, alog, dexp, nw, expand, *, batch, seq):
    q = SSD_CHUNK
    nc = seq // q
    xbc_blk = (N_MAIN - D_XBC) // D_XBC
    rowmap = lambda b, c: (b * nc + c, 0)
    per_layer = lambda b, c: (layer, 0, 0)
    per_batch = lambda b, c: (b, 0, 0)
    return pl.pallas_call(
        functools.partial(_ssd_prompt_kernel, n_chunks=nc),
        grid=(batch, nc),
        in_specs=[
            pl.BlockSpec((q, D_SSM), rowmap),
            pl.BlockSpec((q, D_XBC), lambda b, c: (b * nc + c, xbc_blk)),
            pl.BlockSpec((q, LANES), rowmap),
            pl.BlockSpec((None, D_SSM, STATE), per_batch),
            pl.BlockSpec((None, SUBLANES, D_XBC), per_batch),
            pl.BlockSpec((None, SSD_CONV_W, D_XBC), per_layer),
            pl.BlockSpec((None, 1, D_XBC), per_layer),
            pl.BlockSpec((None, 1, LANES), per_layer),
            pl.BlockSpec((None, 1, LANES), per_layer),
            pl.BlockSpec((None, 1, D_SSM), per_layer),
            pl.BlockSpec((None, 1, D_SSM), per_layer),
            pl.BlockSpec((3 * LANES, D_SSM), lambda b, c: (0, 0)),
        ],
        out_specs=[
            pl.BlockSpec((q, D_SSM), rowmap),
            pl.BlockSpec((None, D_SSM, STATE), per_batch),
            pl.BlockSpec((None, SSD_HIST, D_XBC), per_batch),
        ],
        out_shape=[
            jax.ShapeDtypeStruct((proj.shape[0], D_SSM), BF16),
            jax.ShapeDtypeStruct((batch, D_SSM, STATE), F32),
            jax.ShapeDtypeStruct((batch, SSD_HIST, D_XBC), F32),
        ],
        scratch_shapes=[
            pltpu.VMEM((XBC_SLABS, TILE_GROUP * (q + SUBLANES), LANES), F32),
            pltpu.VMEM((q, D_XBC), F32),
            pltpu.VMEM((q, D_SSM), F32),
            pltpu.VMEM((q, D_SSM), F32),
        ],
        compiler_params=pltpu.CompilerParams(
            dimension_semantics=("parallel", "arbitrary"), vmem_limit_bytes=VMEM_LIMIT),
        name="ssd_prompt",
    )(proj, proj, dtraw, h0, cb8, cw, cbias, dtb, alog, dexp, nw, expand)


N_SSD_SAMPLE_IN = 12


def _ssd_sample_kernel(*refs, n_sub, n_alias):
    (z_ref, xbc_ref, dt_ref, h0_ref, cb_ref, cw_ref, cbias_ref, dtb_ref, alog_ref,
     dexp_ref, nw_ref, e_ref) = refs[:N_SSD_SAMPLE_IN]
    (y_ref, hout_ref, cbout_ref,
     ext_scr, xc_scr, y_scr, xwt_scr, ecx_scr, cd_scr, cum_scr) = refs[N_SSD_SAMPLE_IN + n_alias:]
    q = SSD_CHUNK
    tb, sb, ls = TILE_BATCH, STATE_BATCH, DEC_SEQ
    sub = pl.program_id(1)
    row = lax.broadcasted_iota(jnp.int32, (q, q), 0)
    col = lax.broadcasted_iota(jnp.int32, (q, q), 1)
    same = (row // ls) == (col // ls)
    mask = jnp.logical_and(same, row >= col)

    @pl.when(sub == 0)
    def _token_space():
        cbout_ref[...] = xbc_ref[...].reshape(tb, ls, D_XBC)[:, ls - SSD_HIST:ls, :]
        pad = jnp.zeros((SUBLANES, LANES), F32)
        tok = lax.broadcasted_iota(jnp.int32, (q, LANES), 0) % ls
        for t in range(XBC_TILES):
            slab, j, cols = _tile(t)
            ext_scr[0, slab, _rows(j, 0, SUBLANES), :] = pad
            ext_scr[0, slab, _rows(j, SSD_HIST, q), :] = xbc_ref[:, cols]
            ext_scr[1, slab, _rows(j, 0, q), :] = cb_ref[:, :, cols].reshape(q, LANES)
            ext_scr[1, slab, _rows(j, q, SUBLANES), :] = pad
        for t in range(XBC_TILES):
            slab, j, cols = _tile(t)
            acc = cbias_ref[:, cols] + cw_ref[SSD_HIST:SSD_CONV_W, cols] * xbc_ref[:, cols]
            for s in range(1, SSD_CONV_W):
                k = SSD_HIST - s
                win = jnp.where(tok >= s, ext_scr[0, slab, _rows(j, k, q), :],
                                ext_scr[1, slab, _rows(j, k, q), :])
                acc = acc + cw_ref[k:k + 1, cols] * win
            xc_scr[:, cols] = _silu(acc)

        dt, a = _dt_terms(dt_ref, dtb_ref, alog_ref)
        cum = jnp.dot(mask.astype(F32), a, precision=HIGHEST, preferred_element_type=F32)
        tot = jnp.dot(same.astype(F32), a, precision=HIGHEST, preferred_element_type=F32)
        cum_t = cum.T
        dt_t = dt.T
        cum_scr[0] = cum * LOG2E
        cum_scr[1] = cum_t * LOG2E
        cum_scr[2] = dt_t
        tot_t = tot.T
        wend_t = jnp.exp(tot_t - cum_t) * dt_t
        cdec_t = jnp.exp(tot_t)
        for b in range(tb):
            cd_scr[b] = jnp.broadcast_to(cdec_t[:, b * ls:b * ls + 1], (LANES, STATE))
        ecx_scr[...] = _expand_heads(jnp.exp(cum), e_ref)

        for p in range(HEADS // 2):
            pr = p * LANES
            pair = xc_scr[:, pr:pr + LANES]
            y_scr[:, pr:pr + LANES] = dexp_ref[:, pr:pr + LANES] * pair
            pair_t = pair.T
            for hh in range(2):
                h = 2 * p + hh
                xwt_scr[h * HEAD_DIM:(h + 1) * HEAD_DIM, :] = (
                    pair_t[hh * HEAD_DIM:(hh + 1) * HEAD_DIM, :] * wend_t[h:h + 1, :]).astype(BF16)

    for g in range(GROUPS):
        @pl.when(sub == g)
        def _group_y_diag(g=g):
            bcols = slice(D_SSM + g * STATE, D_SSM + (g + 1) * STATE)
            ccols = slice(D_SSM + (GROUPS + g) * STATE, D_SSM + (GROUPS + g + 1) * STATE)
            cb = _nt_dot(xc_scr[:, ccols].astype(BF16), xc_scr[:, bcols].astype(BF16))
            cum2, cum2_t, dt_t = cum_scr[0], cum_scr[1], cum_scr[2]
            for k in range(HEADS_PER_GROUP // 2):
                h1 = g * HEADS_PER_GROUP + 2 * k
                pr = h1 * HEAD_DIM
                pair = xc_scr[:, pr:pr + LANES]
                y_scr[:, pr:pr + LANES] = (y_scr[:, pr:pr + LANES]
                                           + _pair_y_diag(cb, cum2, cum2_t, dt_t, mask, h1, pair))

    nrow = sb * ls
    rows = pl.ds(pl.multiple_of(sub * nrow, nrow), nrow)
    local_seq = lax.broadcasted_iota(jnp.int32, (nrow, STATE), 0) // ls
    tile_seq = lax.broadcasted_iota(jnp.int32, (q, STATE), 0) // ls
    for g in range(GROUPS):
        gcols = slice(g * GROUP_W, (g + 1) * GROUP_W)
        bcols = slice(D_SSM + g * STATE, D_SSM + (g + 1) * STATE)
        ccols = slice(D_SSM + (GROUPS + g) * STATE, D_SSM + (GROUPS + g + 1) * STATE)
        cg = xc_scr[rows, ccols]
        zero = jnp.zeros_like(cg)
        lhs = jnp.concatenate([jnp.where(local_seq == b, cg, zero) for b in range(sb)],
                              axis=1).astype(BF16)
        hcat = jnp.concatenate([h0_ref[b, gcols, :].astype(BF16) for b in range(sb)], axis=1)
        y_off = _nt_dot(lhs, hcat)
        y_scr[rows, gcols] = y_scr[rows, gcols] + y_off * ecx_scr[rows, gcols]

        bg = xc_scr[:, bcols]
        bzero = jnp.zeros_like(bg)
        bsel = jnp.concatenate([jnp.where(tile_seq == sub * sb + b, bg, bzero) for b in range(sb)],
                               axis=1).astype(BF16)
        s_new = jnp.dot(xwt_scr[gcols, :], bsel, preferred_element_type=F32)
        for b in range(sb):
            cdb = cd_scr[sub * sb + b]
            for hl in range(HEADS_PER_GROUP):
                h = g * HEADS_PER_GROUP + hl
                hrows = slice(h * HEAD_DIM, (h + 1) * HEAD_DIM)
                hout_ref[b, hrows, :] = (
                    h0_ref[b, hrows, :] * cdb[h:h + 1, :]
                    + s_new[hl * HEAD_DIM:(hl + 1) * HEAD_DIM, b * STATE:(b + 1) * STATE])

    @pl.when(sub == n_sub - 1)
    def _finish():
        _gate_norm_store(y_scr, z_ref, nw_ref, y_ref)


def _ssd_sample(proj, dtraw, h0, cb0, y_full, h_stack, layer, cw, cbias, dtb, alog, dexp, nw, expand,
                *, batch, row0):
    q = SSD_CHUNK
    tb, sb = TILE_BATCH, STATE_BATCH
    n_sub = tb // sb
    assert n_sub == GROUPS
    n_tiles = batch // tb
    rb0 = row0 // q
    xbc_blk = (N_MAIN - D_XBC) // D_XBC
    rowmap = lambda i, s: (rb0 + i, 0)
    per_layer = lambda i, s: (layer, 0, 0)
    aliased = [y_full] + ([] if h_stack is None else [h_stack])
    aliases = {N_SSD_SAMPLE_IN: 0}
    if h_stack is not None:
        aliases[N_SSD_SAMPLE_IN + 1] = 1
    return pl.pallas_call(
        functools.partial(_ssd_sample_kernel, n_sub=n_sub, n_alias=len(aliased)),
        grid=(n_tiles, n_sub),
        input_output_aliases=aliases,
        in_specs=[
            pl.BlockSpec((q, D_SSM), rowmap),
            pl.BlockSpec((q, D_XBC), lambda i, s: (rb0 + i, xbc_blk)),
            pl.BlockSpec((q, LANES), rowmap),
            pl.BlockSpec((None, sb, D_SSM, STATE), lambda i, s: (layer, i * n_sub + s, 0, 0)),
            pl.BlockSpec((None, tb, SUBLANES, D_XBC), lambda i, s: (layer, i, 0, 0)),
            pl.BlockSpec((None, SSD_CONV_W, D_XBC), per_layer),
            pl.BlockSpec((None, 1, D_XBC), per_layer),
            pl.BlockSpec((None, 1, LANES), per_layer),
            pl.BlockSpec((None, 1, LANES), per_layer),
            pl.BlockSpec((None, 1, D_SSM), per_layer),
            pl.BlockSpec((None, 1, D_SSM), per_layer),
            pl.BlockSpec((3 * LANES, D_SSM), lambda i, s: (0, 0)),
        ] + [pl.BlockSpec(memory_space=pl.ANY) for _ in aliased],
        out_specs=[
            pl.BlockSpec((q, D_SSM), rowmap),
            pl.BlockSpec((None, sb, D_SSM, STATE), lambda i, s: (layer, i * n_sub + s, 0, 0)),
            pl.BlockSpec((tb, SSD_HIST, D_XBC), lambda i, s: (i, 0, 0)),
        ],
        out_shape=[
            jax.ShapeDtypeStruct(y_full.shape, BF16),
            jax.ShapeDtypeStruct(h0.shape, F32),
            jax.ShapeDtypeStruct((batch, SSD_HIST, D_XBC), F32),
        ],
        scratch_shapes=[
            pltpu.VMEM((2, XBC_SLABS, TILE_GROUP * (q + SUBLANES), LANES), F32),
            pltpu.VMEM((q, D_XBC), F32),
            pltpu.VMEM((q, D_SSM), F32),
            pltpu.VMEM((D_SSM, q), BF16),
            pltpu.VMEM((q, D_SSM), F32),
            pltpu.VMEM((tb, LANES, STATE), F32),
            pltpu.VMEM((3, q, LANES), F32),
        ],
        compiler_params=pltpu.CompilerParams(
            dimension_semantics=("parallel", "arbitrary"), vmem_limit_bytes=VMEM_LIMIT),
        name="ssd_sample",
    )(proj, proj, dtraw, h0, cb0, cw, cbias, dtb, alog, dexp, nw, expand, *aliased)


def _layernorm_silu_store(vc_tiles, lnw_ref, lnb_ref, y_ref):
    total = vc_tiles[0]
    for v in vc_tiles[1:]:
        total = total + v
    mu = total.sum(axis=-1, keepdims=True) * (1.0 / D_CONV)
    sq = None
    for v in vc_tiles:
        d = v - mu
        sq = d * d if sq is None else sq + d * d
    rstd = lax.rsqrt(sq.sum(axis=-1, keepdims=True) * (1.0 / D_CONV) + EPS)
    for t, v in enumerate(vc_tiles):
        cols = slice(t * LANES, (t + 1) * LANES)
        o = ((v - mu) * rstd) * lnw_ref[:, cols] + lnb_ref[:, cols]
        y_ref[:, cols] = _silu(o).astype(y_ref.dtype)


def _conf_prompt_kernel(a_ref, b_ref, buf_ref, w_ref, bias_ref, lnw_ref, lnb_ref, y_ref, bufout_ref,
                        ext_scr, vc_scr, *, n_steps):
    r = CONF_ROWS
    c = pl.program_id(1)
    acc_rows = SSD_CHUNK

    @pl.when(c == 0)
    def _init():
        for t in range(CONV_TILES):
            slab, j, cols = _tile(t)
            ext_scr[slab, _rows(j, 0, CONF_PAD), :] = buf_ref[:, cols]

    for t in range(CONV_TILES):
        slab, j, cols = _tile(t)
        ext_scr[slab, _rows(j, CONF_HIST, r), :] = a_ref[:, cols] * _sigmoid(b_ref[:, cols])

    def slab_body(slab, carry):
        for j in range(TILE_GROUP):
            for rb in range(r // acc_rows):
                base = rb * acc_rows
                acc = bias_ref[slab, j:j + 1, :] + w_ref[slab, 0, j:j + 1, :] * ext_scr[slab, _rows(j, base, acc_rows), :]
                for k in range(1, CONF_W):
                    acc = acc + w_ref[slab, k, j:j + 1, :] * ext_scr[slab, _rows(j, base + k, acc_rows), :]
                vc_scr[slab, j, base:base + acc_rows, :] = acc
        return carry

    lax.fori_loop(0, CONV_SLABS, slab_body, 0)

    @pl.when(c == n_steps - 1)
    def _state():
        for t in range(CONV_TILES):
            slab, j, cols = _tile(t)
            bufout_ref[:, cols] = ext_scr[slab, _rows(j, r, CONF_PAD), :][:CONF_HIST]

    for t in range(CONV_TILES):
        slab, j, cols = _tile(t)
        ext_scr[slab, _rows(j, 0, CONF_PAD), :] = ext_scr[slab, _rows(j, r, CONF_PAD), :]

    _layernorm_silu_store([vc_scr[t // TILE_GROUP, t % TILE_GROUP] for t in range(CONV_TILES)],
                          lnw_ref, lnb_ref, y_ref)


def _conf_prompt(proj, buf32, layer, w_tiles, bias_tiles, lnw, lnb, *, batch, seq):
    r = CONF_ROWS
    ns = seq // r
    per_layer = lambda b, c: (layer, 0, 0)
    return pl.pallas_call(
        functools.partial(_conf_prompt_kernel, n_steps=ns),
        grid=(batch, ns),
        in_specs=[
            pl.BlockSpec((r, D_CONV), lambda b, c: (b * ns + c, 1)),
            pl.BlockSpec((r, D_CONV), lambda b, c: (b * ns + c, 2)),
            pl.BlockSpec((None, CONF_PAD, D_CONV), lambda b, c: (b, 0, 0)),
            pl.BlockSpec((None, CONV_SLABS, CONF_W, TILE_GROUP, LANES), lambda b, c: (layer, 0, 0, 0, 0)),
            pl.BlockSpec((None, CONV_SLABS, TILE_GROUP, LANES), lambda b, c: (layer, 0, 0, 0)),
            pl.BlockSpec((None, 1, D_CONV), per_layer),
            pl.BlockSpec((None, 1, D_CONV), per_layer),
        ],
        out_specs=[
            pl.BlockSpec((r, D_CONV), lambda b, c: (b * ns + c, 0)),
            pl.BlockSpec((None, CONF_HIST, D_CONV), lambda b, c: (b, 0, 0)),
        ],
        out_shape=[
            jax.ShapeDtypeStruct((proj.shape[0], D_CONV), BF16),
            jax.ShapeDtypeStruct((batch, CONF_HIST, D_CONV), F32),
        ],
        scratch_shapes=[
            pltpu.VMEM((CONV_SLABS, TILE_GROUP * (r + CONF_PAD), LANES), F32),
            pltpu.VMEM((CONV_SLABS, TILE_GROUP, r, LANES), F32),
        ],
        compiler_params=pltpu.CompilerParams(
            dimension_semantics=("parallel", "arbitrary"), vmem_limit_bytes=VMEM_LIMIT),
        name="conf_prompt",
    )(proj, proj, buf32, w_tiles, bias_tiles, lnw, lnb)


N_CONF_SAMPLE_IN = 7
CONF_SEQ_ROWS = CONF_HIST + DEC_SEQ + 2


def _conf_sample_kernel(*refs, n_alias):
    a_ref, b_ref, buf_ref, w_ref, bias_ref, lnw_ref, lnb_ref = refs[:N_CONF_SAMPLE_IN]
    y_ref, bufout_ref, ext_scr, vc_scr = refs[N_CONF_SAMPLE_IN + n_alias:]
    tb, ls = TILE_BATCH, DEC_SEQ
    full_rows = (CONF_HIST // SUBLANES) * SUBLANES
    rest = CONF_HIST - full_rows

    for t in range(CONV_TILES):
        slab, j, cols = _tile(t)
        v = a_ref[:, cols] * _sigmoid(b_ref[:, cols])
        for b in range(tb):
            ext_scr[slab, b, _rows(j, 0, full_rows), :] = buf_ref[b, 0:full_rows, cols]
            ext_scr[slab, b, _rows(j, full_rows, rest), :] = buf_ref[b, full_rows:CONF_HIST, cols]
            ext_scr[slab, b, _rows(j, CONF_HIST, ls), :] = v[b * ls:(b + 1) * ls, :]

    def slab_body(slab, carry):
        for j in range(TILE_GROUP):
            accs = [bias_ref[slab, j:j + 1, :] + w_ref[slab, 0, j:j + 1, :] * ext_scr[slab, b, _rows(j, 0, ls), :]
                    for b in range(tb)]
            for k in range(1, CONF_W):
                wk = w_ref[slab, k, j:j + 1, :]
                accs = [acc + wk * ext_scr[slab, b, _rows(j, k, ls), :] for b, acc in enumerate(accs)]
            for b in range(tb):
                vc_scr[slab, j, b * ls:(b + 1) * ls, :] = accs[b]
        return carry

    lax.fori_loop(0, CONV_SLABS, slab_body, 0)

    for t in range(CONV_TILES):
        slab, j, cols = _tile(t)
        for b in range(tb):
            bufout_ref[b, 0:full_rows, cols] = ext_scr[slab, b, _rows(j, ls, full_rows), :]
            bufout_ref[b, full_rows:CONF_HIST, cols] = ext_scr[slab, b, _rows(j, ls + full_rows, rest), :]

    _layernorm_silu_store([vc_scr[t // TILE_GROUP, t % TILE_GROUP] for t in range(CONV_TILES)],
                          lnw_ref, lnb_ref, y_ref)


def _conf_sample(proj, buf, y_full, buf_stack, layer, w_tiles, bias_tiles, lnw, lnb, *, batch, row0):
    tb, ls = TILE_BATCH, DEC_SEQ
    rows = tb * ls
    rb0 = row0 // rows
    per_layer = lambda i: (layer, 0, 0)
    aliased = [y_full] + ([] if buf_stack is None else [buf_stack])
    aliases = {N_CONF_SAMPLE_IN: 0}
    if buf_stack is not None:
        aliases[N_CONF_SAMPLE_IN + 1] = 1
    return pl.pallas_call(
        functools.partial(_conf_sample_kernel, n_alias=len(aliased)),
        grid=(batch // tb,),
        input_output_aliases=aliases,
        in_specs=[
            pl.BlockSpec((rows, D_CONV), lambda i: (rb0 + i, 1)),
            pl.BlockSpec((rows, D_CONV), lambda i: (rb0 + i, 2)),
            pl.BlockSpec((None, tb, CONF_HIST, D_CONV), lambda i: (layer, i, 0, 0)),
            pl.BlockSpec((None, CONV_SLABS, CONF_W, TILE_GROUP, LANES), lambda i: (layer, 0, 0, 0, 0)),
            pl.BlockSpec((None, CONV_SLABS, TILE_GROUP, LANES), lambda i: (layer, 0, 0, 0)),
            pl.BlockSpec((None, 1, D_CONV), per_layer),
            pl.BlockSpec((None, 1, D_CONV), per_layer),
        ] + [pl.BlockSpec(memory_space=pl.ANY) for _ in aliased],
        out_specs=[
            pl.BlockSpec((rows, D_CONV), lambda i: (rb0 + i, 0)),
            pl.BlockSpec((None, tb, CONF_HIST, D_CONV), lambda i: (layer, i, 0, 0)),
        ],
        out_shape=[
            jax.ShapeDtypeStruct(y_full.shape, BF16),
            jax.ShapeDtypeStruct(buf.shape, F32),
        ],
        scratch_shapes=[
            pltpu.VMEM((CONV_SLABS, tb, TILE_GROUP * CONF_SEQ_ROWS, LANES), F32),
            pltpu.VMEM((CONV_SLABS, TILE_GROUP, rows, LANES), F32),
        ],
        compiler_params=pltpu.CompilerParams(
            dimension_semantics=("parallel",), vmem_limit_bytes=VMEM_LIMIT),
        name="conf_sample",
    )(proj, proj, buf, w_tiles, bias_tiles, lnw, lnb, *aliased)


def kernel(x_prompt, x_sample, state_ssm, state_ssd_conv, state_conformer_conv, w_in, ssd_conv_w, ssd_conv_b,
           dt_bias, a_log, d_skip, ssd_norm_w, conf_conv_w, conf_conv_b, conf_norm_w, conf_norm_b, w_out,
           norm_pre_mix, norm_post_mix, norm_pre_mlp, norm_post_mlp, w_up, w_down):
    pb, pl_len, d = x_prompt.shape
    sbatch, s_len, _ = x_sample.shape
    tp = pb * pl_len
    ts = sbatch * s_len
    depth = w_in.shape[0]

    w_glu = w_in[..., DT_END:].astype(BF16)
    w_dt = jnp.pad(w_in[..., XBC_END:DT_END], ((0, 0), (0, 0), (0, LANES - HEADS))).astype(BF16)
    w_out16 = w_out.astype(BF16)
    w_down16 = w_down.astype(BF16)

    vec = lambda a: a.reshape(depth, 1, a.shape[-1])
    pad_heads = lambda a: jnp.pad(a, ((0, 0), (0, LANES - HEADS))).reshape(depth, 1, LANES)
    dtb = pad_heads(dt_bias)
    alog = pad_heads(a_log)
    dexp = vec(jnp.repeat(d_skip, HEAD_DIM, axis=-1))
    cbias = vec(ssd_conv_b)
    ssd_nw = vec(ssd_norm_w)
    conf_w_tiles = conf_conv_w.reshape(depth, CONF_W, CONV_SLABS, TILE_GROUP, LANES).transpose(0, 2, 1, 3, 4)
    conf_b_tiles = conf_conv_b.reshape(depth, CONV_SLABS, TILE_GROUP, LANES)
    lnw = vec(conf_norm_w)
    lnb = vec(conf_norm_b)
    n_pre_mix, n_post_mix = vec(norm_pre_mix), vec(norm_post_mix)
    n_pre_mlp, n_post_mlp = vec(norm_pre_mlp), vec(norm_post_mlp)
    expand = (lax.broadcasted_iota(jnp.int32, (LANES, D_SSM), 1) // HEAD_DIM
              == lax.broadcasted_iota(jnp.int32, (LANES, D_SSM), 0)).astype(BF16)
    expand = jnp.tile(expand, (3, 1))

    h0_sample = state_ssm.reshape(depth, sbatch, D_SSM, STATE)
    cb0_sample = jnp.pad(state_ssd_conv, ((0, 0), (0, 0), (0, SUBLANES - SSD_HIST), (0, 0)))
    h0_prompt = jnp.zeros((pb, D_SSM, STATE), F32)
    cb0_prompt = jnp.zeros((pb, SUBLANES, D_XBC), F32)
    conf0_prompt = jnp.zeros((pb, CONF_PAD, D_CONV), F32)

    x, u, dtraw = _entry_call(x_prompt.reshape(tp, d), x_sample.reshape(ts, d), n_pre_mix, w_dt)

    tn = 1024
    z_blocks, glu_blocks, xbc_blocks = D_SSM // tn, 2 * D_CONV // tn, D_XBC // tn
    p_ssm, p_sc, p_cc, s_sc = [], [], [], []
    s_ssm = s_cc = None
    for i in range(depth):
        proj = _matmul(u, w_in, i, tm=1024, tn=tn, n_blocks=z_blocks + xbc_blocks, out_cols=N_MAIN,
                       out_col=lambda j: jnp.where(j < z_blocks, j, j + glu_blocks), name="in_proj_zx")
        proj = _matmul(u, w_glu, i, tm=1024, tn=tn, n_blocks=glu_blocks, into=proj,
                       out_col=lambda j: j + z_blocks, name="in_proj_glu")
        ssd_y, h_p, sc_p = _ssd_prompt(proj, dtraw, h0_prompt, cb0_prompt, i, ssd_conv_w, cbias, dtb, alog,
                                       dexp, ssd_nw, expand, batch=pb, seq=pl_len)
        ssd_y, s_ssm, sc_s = _ssd_sample(proj, dtraw, h0_sample, cb0_sample, ssd_y, s_ssm, i, ssd_conv_w,
                                         cbias, dtb, alog, dexp, ssd_nw, expand, batch=sbatch, row0=tp)
        conf_y, cc_p = _conf_prompt(proj, conf0_prompt, i, conf_w_tiles, conf_b_tiles, lnw, lnb,
                                    batch=pb, seq=pl_len)
        conf_y, s_cc = _conf_sample(proj, state_conformer_conv, conf_y, s_cc, i, conf_w_tiles, conf_b_tiles,
                                    lnw, lnb, batch=sbatch, row0=tp)
        x, u2 = _proj_norm([ssd_y, conf_y], w_out16, i, x, n_post_mix, n_pre_mlp, None, i, name="out_proj")
        hid = _matmul(u2, w_up, i, tm=1024, tn=tn, n_blocks=D_FF // tn, relu2=True, out_dtype=BF16,
                      name="up_proj")
        if i + 1 < depth:
            x, u, dtraw = _proj_norm([hid], w_down16, i, x, n_post_mlp, n_pre_mix, w_dt, i + 1,
                                     name="down_proj")
        else:
            y_p, y_s = _proj_norm([hid], w_down16, i, x, n_post_mlp, None, None, None, split_rows=tp,
                                  name="down_proj")
        p_ssm.append(h_p)
        p_sc.append(sc_p)
        p_cc.append(cc_p)
        s_sc.append(sc_s)

    y_prompt = y_p.reshape(pb, pl_len, d)
    y_sample = y_s.reshape(sbatch, s_len, d)
    state_shape = lambda b: (depth, b, HEADS, HEAD_DIM, STATE)
    return (y_prompt, y_sample,
            jnp.stack(p_ssm).reshape(state_shape(pb)), jnp.stack(p_sc), jnp.stack(p_cc),
            s_ssm.reshape(state_shape(sbatch)), jnp.stack(s_sc), s_cc)
```

```python
import functools

import jax
import jax.numpy as jnp
from jax import lax
from jax.experimental import pallas as pl
from jax.experimental.pallas import tpu as pltpu

F32 = jnp.float32
BF16 = jnp.bfloat16
HIGHEST = lax.Precision.HIGHEST

D_MODEL = 2048
DEPTH = 4
D_SSM = 2048
D_CONV = 2048
HEAD_DIM = 64
HEADS = 32
GROUPS = 4
HEADS_PER_GROUP = HEADS // GROUPS
GROUP_W = D_SSM // GROUPS
STATE = 128
SSD_CONV_W = 4
SSD_HIST = SSD_CONV_W - 1
D_XBC = D_SSM + 2 * GROUPS * STATE
CONF_W = 31
CONF_HIST = CONF_W - 1
D_FF = 4 * D_MODEL
EPS = 1e-6

LANES = 128
SUBLANES = 8
TILE_GROUP = 4

N_MAIN = D_SSM + 2 * D_CONV + D_XBC
Z_END = D_SSM
XBC_END = Z_END + D_XBC
DT_END = XBC_END + HEADS

SSD_CHUNK = 128
DEC_SEQ = 8
TILE_BATCH = SSD_CHUNK // DEC_SEQ
STATE_BATCH = 4
CONF_ROWS = 256
CONF_PAD = 32
CONV_TILES = D_CONV // LANES
XBC_TILES = D_XBC // LANES
CONV_SLABS = CONV_TILES // TILE_GROUP
XBC_SLABS = XBC_TILES // TILE_GROUP

VMEM_LIMIT = 56 * 1024 * 1024


def _sigmoid(x):
    return 0.5 + 0.5 * jnp.tanh(0.5 * x)


def _silu(x):
    h = 0.5 * x
    return h + h * jnp.tanh(h)


def _softplus(x):
    return jnp.maximum(x, 0.0) + jnp.log1p(jnp.exp(-jnp.abs(x)))


def _rms_scale(x):
    return x * lax.rsqrt(jnp.mean(x * x, axis=-1, keepdims=True) + EPS)


def _tile(t):
    return t // TILE_GROUP, t % TILE_GROUP, slice(t * LANES, (t + 1) * LANES)


def _rows(j, start, n):
    return pl.ds(TILE_GROUP * start + j, n, stride=TILE_GROUP)


def _entry_kernel(xp_ref, xs_ref, wnext_ref, wdt_ref, xo_ref, u_ref, dt_ref, *, prompt_blocks):
    def emit(x):
        xo_ref[...] = x
        u = (_rms_scale(x) * wnext_ref[...]).astype(BF16)
        u_ref[...] = u
        dt_ref[...] = jnp.dot(u, wdt_ref[...], preferred_element_type=F32)

    i = pl.program_id(0)

    @pl.when(i < prompt_blocks)
    def _prompt():
        emit(xp_ref[...])

    @pl.when(i >= prompt_blocks)
    def _sample():
        emit(xs_ref[...])


def _entry_call(xp, xs, wnext, wdt, *, rows=256):
    tp, d = xp.shape
    ts = xs.shape[0]
    t = tp + ts
    pblocks = tp // rows
    row_spec = pl.BlockSpec((rows, d), lambda i: (i, 0))
    return pl.pallas_call(
        functools.partial(_entry_kernel, prompt_blocks=pblocks),
        grid=(t // rows,),
        in_specs=[
            pl.BlockSpec((rows, d), lambda i: (jnp.minimum(i, pblocks - 1), 0)),
            pl.BlockSpec((rows, d), lambda i: (jnp.maximum(i - pblocks, 0), 0)),
            pl.BlockSpec((None, 1, d), lambda i: (0, 0, 0)),
            pl.BlockSpec((None, d, LANES), lambda i: (0, 0, 0)),
        ],
        out_specs=[row_spec, row_spec, pl.BlockSpec((rows, LANES), lambda i: (i, 0))],
        out_shape=[jax.ShapeDtypeStruct((t, d), F32), jax.ShapeDtypeStruct((t, d), BF16),
                   jax.ShapeDtypeStruct((t, LANES), F32)],
        compiler_params=pltpu.CompilerParams(
            dimension_semantics=("arbitrary",), vmem_limit_bytes=VMEM_LIMIT),
        name="entry_norm",
    )(xp, xs, wnext, wdt)


def _up_proj_kernel(x_ref, w_ref, o_ref, wbf_scr):
    @pl.when(pl.program_id(1) == 0)
    def _round_weights():
        wbf_scr[...] = w_ref[...].astype(BF16)

    h = jnp.maximum(jnp.dot(x_ref[...], wbf_scr[...], preferred_element_type=F32), 0.0)
    o_ref[...] = (h * h).astype(o_ref.dtype)


def _up_proj(x, w, layer, *, tm, tn):
    m, k = x.shape
    n = w.shape[-1]
    return pl.pallas_call(
        _up_proj_kernel,
        grid=(n // tn, m // tm),
        in_specs=[pl.BlockSpec((tm, k), lambda j, i: (i, 0)),
                  pl.BlockSpec((None, k, tn), lambda j, i: (layer, 0, j))],
        out_specs=pl.BlockSpec((tm, tn), lambda j, i: (i, j)),
        out_shape=jax.ShapeDtypeStruct((m, n), BF16),
        scratch_shapes=[pltpu.VMEM((k, tn), BF16)],
        compiler_params=pltpu.CompilerParams(
            dimension_semantics=("parallel", "arbitrary"), vmem_limit_bytes=VMEM_LIMIT),
        name="up_proj",
    )(x, w)


def _in_proj_kernel(x_ref, wt_ref, o_ref, wbf_scr):
    @pl.when(pl.program_id(1) == 0)
    def _round_weights():
        wbf_scr[...] = wt_ref[0].T.astype(BF16)

    o_ref[...] = jnp.dot(x_ref[...], wbf_scr[...], preferred_element_type=F32)


def _in_proj(x, w_t, layer, *, tm, tn):
    m, k = x.shape
    z_blocks, glu_blocks = D_SSM // tn, 2 * D_CONV // tn

    def w_row(j):
        row = jnp.where(j < z_blocks, j * tn,
                        jnp.where(j < z_blocks + glu_blocks, DT_END + (j - z_blocks) * tn,
                                  Z_END + (j - z_blocks - glu_blocks) * tn))
        return pl.multiple_of(row, HEADS)

    return pl.pallas_call(
        _in_proj_kernel,
        grid=(N_MAIN // tn, m // tm),
        in_specs=[pl.BlockSpec((tm, k), lambda j, i: (i, 0)),
                  pl.BlockSpec((pl.Element(1), pl.Element(tn), pl.Element(k)),
                               lambda j, i: (layer, w_row(j), 0))],
        out_specs=pl.BlockSpec((tm, tn), lambda j, i: (i, j)),
        out_shape=jax.ShapeDtypeStruct((m, N_MAIN), F32),
        scratch_shapes=[pltpu.VMEM((k, tn), BF16)],
        compiler_params=pltpu.CompilerParams(
            dimension_semantics=("parallel", "arbitrary"), vmem_limit_bytes=VMEM_LIMIT),
        name="in_proj",
    )(x, w_t)


def _proj_norm_kernel(*refs, part_steps, has_next, has_dt, split_block):
    n_parts = len(part_steps)
    x_refs = refs[:n_parts]
    w_ref, res_ref, wpost_ref = refs[n_parts:n_parts + 3]
    pos = n_parts + 3
    wnext_ref = wdt_ref = u_ref = dt_ref = None
    if has_next:
        wnext_ref = refs[pos]
        pos += 1
    if has_dt:
        wdt_ref = refs[pos]
        pos += 1
    n_xo = 1 if split_block is None else 2
    xo_refs = refs[pos:pos + n_xo]
    pos += n_xo
    if has_next:
        u_ref = refs[pos]
        pos += 1
    if has_dt:
        dt_ref = refs[pos]
        pos += 1
    acc_ref = refs[pos]

    i = pl.program_id(0)
    k = pl.program_id(1)
    n_k = sum(part_steps)
    start = 0
    for xr, steps in zip(x_refs, part_steps):
        @pl.when(jnp.logical_and(k >= start, k < start + steps))
        def _accumulate(xr=xr):
            d = jnp.dot(xr[...], w_ref[...], preferred_element_type=F32)

            @pl.when(k == 0)
            def _first():
                acc_ref[...] = d

            @pl.when(k != 0)
            def _rest():
                acc_ref[...] += d

        start += steps

    @pl.when(k == n_k - 1)
    def _epilogue():
        x = res_ref[...] + _rms_scale(acc_ref[...]) * wpost_ref[...]
        if split_block is None:
            xo_refs[0][...] = x
        else:
            @pl.when(i < split_block)
            def _head_rows():
                xo_refs[0][...] = x

            @pl.when(i >= split_block)
            def _tail_rows():
                xo_refs[1][...] = x
        if has_next:
            u = (_rms_scale(x) * wnext_ref[...]).astype(BF16)
            u_ref[...] = u
            if has_dt:
                dt_ref[...] = jnp.dot(u, wdt_ref[...], preferred_element_type=F32)


def _proj_norm(xs, w, layer, res, wpost, wnext, wdt, next_layer, *, split_rows=None, tm=512, tk=2048,
               name="proj_norm"):
    m, d = res.shape
    part_steps = tuple(x.shape[1] // tk for x in xs)
    n_k = sum(part_steps)
    has_next, has_dt = wnext is not None, wdt is not None
    in_specs, start = [], 0
    for steps in part_steps:
        in_specs.append(pl.BlockSpec(
            (tm, tk), functools.partial(lambda i, k, s, n: (i, jnp.clip(k - s, 0, n - 1)), s=start, n=steps)))
        start += steps
    row_spec = pl.BlockSpec((tm, d), lambda i, k: (i, 0))
    in_specs += [pl.BlockSpec((None, tk, d), lambda i, k: (layer, k, 0)), row_spec,
                 pl.BlockSpec((None, 1, d), lambda i, k: (layer, 0, 0))]
    args = [*xs, w, res, wpost]
    if has_next:
        in_specs.append(pl.BlockSpec((None, 1, d), lambda i, k: (next_layer, 0, 0)))
        args.append(wnext)
    if has_dt:
        in_specs.append(pl.BlockSpec((None, d, LANES), lambda i, k: (next_layer, 0, 0)))
        args.append(wdt)
    if split_rows is None:
        split_block = None
        out_specs = [row_spec]
        out_shape = [jax.ShapeDtypeStruct((m, d), F32)]
    else:
        split_block = split_rows // tm
        out_specs = [pl.BlockSpec((tm, d), lambda i, k: (jnp.minimum(i, split_block - 1), 0)),
                     pl.BlockSpec((tm, d), lambda i, k: (jnp.maximum(i - split_block, 0), 0))]
        out_shape = [jax.ShapeDtypeStruct((split_rows, d), F32),
                     jax.ShapeDtypeStruct((m - split_rows, d), F32)]
    if has_next:
        out_specs.append(row_spec)
        out_shape.append(jax.ShapeDtypeStruct((m, d), BF16))
    if has_dt:
        out_specs.append(pl.BlockSpec((tm, LANES), lambda i, k: (i, 0)))
        out_shape.append(jax.ShapeDtypeStruct((m, LANES), F32))
    return pl.pallas_call(
        functools.partial(_proj_norm_kernel, part_steps=part_steps, has_next=has_next, has_dt=has_dt,
                          split_block=split_block),
        grid=(m // tm, n_k),
        in_specs=in_specs,
        out_specs=out_specs,
        out_shape=out_shape,
        scratch_shapes=[pltpu.VMEM((tm, d), F32)],
        compiler_params=pltpu.CompilerParams(
            dimension_semantics=("arbitrary", "arbitrary"), vmem_limit_bytes=VMEM_LIMIT),
        name=name,
    )(*args)


def _nt_dot(a, b):
    return lax.dot_general(a, b, (((1,), (1,)), ((), ())), preferred_element_type=F32)


LOG2E = 1.4426950408889634


def _head_decay_weights(cb, cum2, cum2_t, dt_t, mask, h):
    seg2 = cum2[:, h:h + 1] - cum2_t[h:h + 1, :]
    decay = jnp.exp2(jnp.where(mask, seg2, -jnp.inf)) * dt_t[h:h + 1, :]
    return (cb * decay).astype(BF16)


def _pair_y_diag(cb, cum2, cum2_t, dt_t, mask, h1, pair):
    q = pair.shape[0]
    w_stack = jnp.concatenate([_head_decay_weights(cb, cum2, cum2_t, dt_t, mask, h1),
                               _head_decay_weights(cb, cum2, cum2_t, dt_t, mask, h1 + 1)], axis=0)
    yy = jnp.dot(w_stack, pair.astype(BF16), preferred_element_type=F32)
    lo = lax.broadcasted_iota(jnp.int32, pair.shape, 1) < HEAD_DIM
    return jnp.where(lo, yy[:q], yy[q:])


def _gate_norm_store(y_scr, z_ref, nw_ref, y_ref):
    for g in range(GROUPS):
        cols = slice(g * GROUP_W, (g + 1) * GROUP_W)
        gated = y_scr[:, cols] * _silu(z_ref[:, cols])
        y_ref[:, cols] = (_rms_scale(gated) * nw_ref[:, cols]).astype(y_ref.dtype)


def _expand_heads(v, e3_ref):
    hi = v.astype(BF16)
    r1 = v - hi.astype(F32)
    mid = r1.astype(BF16)
    lo = (r1 - mid.astype(F32)).astype(BF16)
    return jnp.dot(jnp.concatenate([hi, mid, lo], axis=1), e3_ref[...], preferred_element_type=F32)


def _dt_terms(dt_ref, dtb_ref, alog_ref):
    dt = _softplus(dt_ref[...] + dtb_ref[...])
    a = dt * (-jnp.exp(alog_ref[...]))
    return dt, a


def _ssd_prompt_kernel(z_ref, xbc_ref, dt_ref, h0_ref, cb_ref, cw_ref, cbias_ref, dtb_ref, alog_ref,
                       dexp_ref, nw_ref, e_ref, y_ref, hout_ref, cbout_ref,
                       ext_scr, xc_scr, y_scr, ecx_scr, *, n_chunks):
    q = SSD_CHUNK
    half = HEAD_DIM
    c = pl.program_id(1)

    @pl.when(c == 0)
    def _init():
        hout_ref[...] = h0_ref[...]
        for t in range(XBC_TILES):
            slab, j, cols = _tile(t)
            ext_scr[slab, _rows(j, 0, SUBLANES), :] = cb_ref[:, cols]

    dt, a = _dt_terms(dt_ref, dtb_ref, alog_ref)
    row = lax.broadcasted_iota(jnp.int32, (q, q), 0)
    col = lax.broadcasted_iota(jnp.int32, (q, q), 1)
    causal = row >= col
    cum = jnp.dot(causal.astype(F32), a, precision=HIGHEST, preferred_element_type=F32)
    cum_t = cum.T
    dt_t = dt.T
    tot = jnp.broadcast_to(cum_t[:, q - 1:q], (LANES, q))
    wend_t = jnp.exp(tot - cum_t) * dt_t
    cdec = jnp.exp(tot)
    ecx_scr[...] = _expand_heads(jnp.exp(cum), e_ref)
    cum2 = cum * LOG2E
    cum2_t = cum_t * LOG2E

    for t in range(XBC_TILES):
        slab, j, cols = _tile(t)
        ext_scr[slab, _rows(j, SSD_HIST, q), :] = xbc_ref[:, cols]
    for t in range(XBC_TILES):
        slab, j, cols = _tile(t)
        acc = cbias_ref[:, cols] + cw_ref[0:1, cols] * ext_scr[slab, _rows(j, 0, q), :]
        for k in range(1, SSD_CONV_W):
            acc = acc + cw_ref[k:k + 1, cols] * ext_scr[slab, _rows(j, k, q), :]
        xc_scr[:, cols] = _silu(acc)
    for t in range(XBC_TILES):
        slab, j, cols = _tile(t)
        ext_scr[slab, _rows(j, 0, SUBLANES), :] = ext_scr[slab, _rows(j, q, SUBLANES), :]

    for g in range(GROUPS):
        gcols = slice(g * GROUP_W, (g + 1) * GROUP_W)
        bcols = slice(D_SSM + g * STATE, D_SSM + (g + 1) * STATE)
        ccols = slice(D_SSM + (GROUPS + g) * STATE, D_SSM + (GROUPS + g + 1) * STATE)
        bg16 = xc_scr[:, bcols].astype(BF16)
        cg16 = xc_scr[:, ccols].astype(BF16)
        cb = _nt_dot(cg16, bg16)
        hg = hout_ref[gcols, :]
        y_off = _nt_dot(cg16, hg.astype(BF16))
        for k in range(HEADS_PER_GROUP // 2):
            h1 = g * HEADS_PER_GROUP + 2 * k
            h2 = h1 + 1
            pr = h1 * HEAD_DIM
            pcols = slice(pr, pr + LANES)
            pair = xc_scr[:, pcols]
            y_scr[:, pcols] = (_pair_y_diag(cb, cum2, cum2_t, dt_t, causal, h1, pair)
                               + y_off[:, k * LANES:(k + 1) * LANES] * ecx_scr[:, pcols]
                               + dexp_ref[:, pcols] * pair)
            pair_t = pair.T
            xw = jnp.concatenate([pair_t[:half] * wend_t[h1:h1 + 1, :],
                                  pair_t[half:] * wend_t[h2:h2 + 1, :]], axis=0).astype(BF16)
            s_new = jnp.dot(xw, bg16, preferred_element_type=F32)
            hp = hg[k * LANES:(k + 1) * LANES]
            hout_ref[pcols, :] = jnp.concatenate([hp[:half] * cdec[h1:h1 + 1, :],
                                                  hp[half:] * cdec[h2:h2 + 1, :]], axis=0) + s_new

    _gate_norm_store(y_scr, z_ref, nw_ref, y_ref)

    @pl.when(c == n_chunks - 1)
    def _conv_state():
        cbout_ref[...] = xbc_ref[q - SSD_HIST:q, :]


def _ssd_prompt(proj, dtraw, h0, cb8, layer, cw, cbias, dtb, alog, dexp, nw, expand, *, batch, seq):
    q = SSD_CHUNK
    nc = seq // q
    xbc_blk = (N_MAIN - D_XBC) // D_XBC
    rowmap = lambda b, c: (b * nc + c, 0)
    per_layer = lambda b, c: (layer, 0, 0)
    per_batch = lambda b, c: (b, 0, 0)
    return pl.pallas_call(
        functools.partial(_ssd_prompt_kernel, n_chunks=nc),
        grid=(batch, nc),
        in_specs=[
            pl.BlockSpec((q, D_SSM), rowmap),
            pl.BlockSpec((q, D_XBC), lambda b, c: (b * nc + c, xbc_blk)),
            pl.BlockSpec((q, LANES), rowmap),
            pl.BlockSpec((None, D_SSM, STATE), per_batch),
            pl.BlockSpec((None, SUBLANES, D_XBC), per_batch),
            pl.BlockSpec((None, SSD_CONV_W, D_XBC), per_layer),
            pl.BlockSpec((None, 1, D_XBC), per_layer),
            pl.BlockSpec((None, 1, LANES), per_layer),
            pl.BlockSpec((None, 1, LANES), per_layer),
            pl.BlockSpec((None, 1, D_SSM), per_layer),
            pl.BlockSpec((None, 1, D_SSM), per_layer),
            pl.BlockSpec((3 * LANES, D_SSM), lambda b, c: (0, 0)),
        ],
        out_specs=[
            pl.BlockSpec((q, D_SSM), rowmap),
            pl.BlockSpec((None, D_SSM, STATE), per_batch),
            pl.BlockSpec((None, SSD_HIST, D_XBC), per_batch),
        ],
        out_shape=[
            jax.ShapeDtypeStruct((proj.shape[0], D_SSM), BF16),
            jax.ShapeDtypeStruct((batch, D_SSM, STATE), F32),
            jax.ShapeDtypeStruct((batch, SSD_HIST, D_XBC), F32),
        ],
        scratch_shapes=[
            pltpu.VMEM((XBC_SLABS, TILE_GROUP * (q + SUBLANES), LANES), F32),
            pltpu.VMEM((q, D_XBC), F32),
            pltpu.VMEM((q, D_SSM), F32),
            pltpu.VMEM((q, D_SSM), F32),
        ],
        compiler_params=pltpu.CompilerParams(
            dimension_semantics=("parallel", "arbitrary"), vmem_limit_bytes=VMEM_LIMIT),
        name="ssd_prompt",
    )(proj, proj, dtraw, h0, cb8, cw, cbias, dtb, alog, dexp, nw, expand)


N_SSD_SAMPLE_IN = 12


def _ssd_sample_kernel(*refs, n_sub, n_alias):
    (z_ref, xbc_ref, dt_ref, h0_ref, cb_ref, cw_ref, cbias_ref, dtb_ref, alog_ref,
     dexp_ref, nw_ref, e_ref) = refs[:N_SSD_SAMPLE_IN]
    (y_ref, hout_ref, cbout_ref,
     ext_scr, xc_scr, y_scr, xwt_scr, ecx_scr, cd_scr, cum_scr) = refs[N_SSD_SAMPLE_IN + n_alias:]
    q = SSD_CHUNK
    tb, sb = TILE_BATCH, STATE_BATCH
    sub = pl.program_id(1)
    row = lax.broadcasted_iota(jnp.int32, (q, q), 0)
    col = lax.broadcasted_iota(jnp.int32, (q, q), 1)
    same = (row % tb) == (col % tb)
    mask = jnp.logical_and(same, row >= col)

    @pl.when(sub == 0)
    def _token_space():
        nh = SSD_HIST * tb
        ext_scr[0:nh, :] = cb_ref[...].reshape(nh, D_XBC)
        ext_scr[nh:nh + q, :] = xbc_ref[...]
        cbout_ref[...] = xbc_ref[q - nh:q, :].reshape(SSD_HIST, tb, D_XBC)
        for t in range(XBC_TILES):
            cols = slice(t * LANES, (t + 1) * LANES)
            acc = cbias_ref[:, cols] + cw_ref[0:1, cols] * ext_scr[0:q, cols]
            for k in range(1, SSD_CONV_W):
                acc = acc + cw_ref[k:k + 1, cols] * ext_scr[k * tb:k * tb + q, cols]
            xc_scr[:, cols] = _silu(acc)

        dt, a = _dt_terms(dt_ref, dtb_ref, alog_ref)
        cum = jnp.dot(mask.astype(F32), a, precision=HIGHEST, preferred_element_type=F32)
        tot = jnp.dot(same.astype(F32), a, precision=HIGHEST, preferred_element_type=F32)
        cum_t = cum.T
        dt_t = dt.T
        cum_scr[0] = cum * LOG2E
        cum_scr[1] = cum_t * LOG2E
        cum_scr[2] = dt_t
        tot_t = tot.T
        wend_t = jnp.exp(tot_t - cum_t) * dt_t
        cdec_t = jnp.exp(tot_t)
        for b in range(tb):
            cd_scr[b] = jnp.broadcast_to(cdec_t[:, b:b + 1], (LANES, STATE))
        ecx_scr[...] = _expand_heads(jnp.exp(cum), e_ref)

        for p in range(HEADS // 2):
            pr = p * LANES
            pair = xc_scr[:, pr:pr + LANES]
            y_scr[:, pr:pr + LANES] = dexp_ref[:, pr:pr + LANES] * pair
            pair_t = pair.T
            for hh in range(2):
                h = 2 * p + hh
                xwt_scr[h * HEAD_DIM:(h + 1) * HEAD_DIM, :] = (
                    pair_t[hh * HEAD_DIM:(hh + 1) * HEAD_DIM, :] * wend_t[h:h + 1, :]).astype(BF16)

    for g in range(GROUPS):
        @pl.when(sub == g)
        def _group_y_diag(g=g):
            bcols = slice(D_SSM + g * STATE, D_SSM + (g + 1) * STATE)
            ccols = slice(D_SSM + (GROUPS + g) * STATE, D_SSM + (GROUPS + g + 1) * STATE)
            cb = _nt_dot(xc_scr[:, ccols].astype(BF16), xc_scr[:, bcols].astype(BF16))
            cum2, cum2_t, dt_t = cum_scr[0], cum_scr[1], cum_scr[2]
            for k in range(HEADS_PER_GROUP // 2):
                h1 = g * HEADS_PER_GROUP + 2 * k
                pr = h1 * HEAD_DIM
                pair = xc_scr[:, pr:pr + LANES]
                y_scr[:, pr:pr + LANES] = (y_scr[:, pr:pr + LANES]
                                           + _pair_y_diag(cb, cum2, cum2_t, dt_t, mask, h1, pair))

    tile_seq = lax.broadcasted_iota(jnp.int32, (q, STATE), 0) % tb
    for g in range(GROUPS):
        gcols = slice(g * GROUP_W, (g + 1) * GROUP_W)
        bcols = slice(D_SSM + g * STATE, D_SSM + (g + 1) * STATE)
        ccols = slice(D_SSM + (GROUPS + g) * STATE, D_SSM + (GROUPS + g + 1) * STATE)
        cg = xc_scr[:, ccols]
        zero = jnp.zeros_like(cg)
        lhs = jnp.concatenate([jnp.where(tile_seq == sub * sb + b, cg, zero) for b in range(sb)],
                              axis=1).astype(BF16)
        hcat = jnp.concatenate([h0_ref[b, gcols, :].astype(BF16) for b in range(sb)], axis=1)
        y_off = _nt_dot(lhs, hcat)
        y_scr[:, gcols] = y_scr[:, gcols] + y_off * ecx_scr[:, gcols]

        bg = xc_scr[:, bcols]
        bzero = jnp.zeros_like(bg)
        bsel = jnp.concatenate([jnp.where(tile_seq == sub * sb + b, bg, bzero) for b in range(sb)],
                               axis=1).astype(BF16)
        s_new = jnp.dot(xwt_scr[gcols, :], bsel, preferred_element_type=F32)
        for b in range(sb):
            cdb = cd_scr[sub * sb + b]
            for hl in range(HEADS_PER_GROUP):
                h = g * HEADS_PER_GROUP + hl
                hrows = slice(h * HEAD_DIM, (h + 1) * HEAD_DIM)
                hout_ref[b, hrows, :] = (
                    h0_ref[b, hrows, :] * cdb[h:h + 1, :]
                    + s_new[hl * HEAD_DIM:(hl + 1) * HEAD_DIM, b * STATE:(b + 1) * STATE])

    @pl.when(sub == n_sub - 1)
    def _finish():
        _gate_norm_store(y_scr, z_ref, nw_ref, y_ref)


def _ssd_sample(proj, dtraw, h0, cb0, y_full, h_stack, layer, cw, cbias, dtb, alog, dexp, nw, expand,
                *, batch, row0):
    q = SSD_CHUNK
    tb, sb = TILE_BATCH, STATE_BATCH
    n_sub = tb // sb
    assert n_sub == GROUPS
    n_tiles = batch // tb
    rb0 = row0 // q
    xbc_blk = (N_MAIN - D_XBC) // D_XBC
    rowmap = lambda i, s: (rb0 + i, 0)
    per_layer = lambda i, s: (layer, 0, 0)
    aliased = [y_full] + ([] if h_stack is None else [h_stack])
    aliases = {N_SSD_SAMPLE_IN: 0}
    if h_stack is not None:
        aliases[N_SSD_SAMPLE_IN + 1] = 1
    return pl.pallas_call(
        functools.partial(_ssd_sample_kernel, n_sub=n_sub, n_alias=len(aliased)),
        grid=(n_tiles, n_sub),
        input_output_aliases=aliases,
        in_specs=[
            pl.BlockSpec((q, D_SSM), rowmap),
            pl.BlockSpec((q, D_XBC), lambda i, s: (rb0 + i, xbc_blk)),
            pl.BlockSpec((q, LANES), rowmap),
            pl.BlockSpec((None, sb, D_SSM, STATE), lambda i, s: (layer, i * n_sub + s, 0, 0)),
            pl.BlockSpec((None, SSD_HIST, tb, D_XBC), lambda i, s: (layer, 0, i, 0)),
            pl.BlockSpec((None, SSD_CONV_W, D_XBC), per_layer),
            pl.BlockSpec((None, 1, D_XBC), per_layer),
            pl.BlockSpec((None, 1, LANES), per_layer),
            pl.BlockSpec((None, 1, LANES), per_layer),
            pl.BlockSpec((None, 1, D_SSM), per_layer),
            pl.BlockSpec((None, 1, D_SSM), per_layer),
            pl.BlockSpec((3 * LANES, D_SSM), lambda i, s: (0, 0)),
        ] + [pl.BlockSpec(memory_space=pl.ANY) for _ in aliased],
        out_specs=[
            pl.BlockSpec((q, D_SSM), rowmap),
            pl.BlockSpec((None, sb, D_SSM, STATE), lambda i, s: (layer, i * n_sub + s, 0, 0)),
            pl.BlockSpec((SSD_HIST, tb, D_XBC), lambda i, s: (0, i, 0)),
        ],
        out_shape=[
            jax.ShapeDtypeStruct(y_full.shape, BF16),
            jax.ShapeDtypeStruct(h0.shape, F32),
            jax.ShapeDtypeStruct((SSD_HIST, batch, D_XBC), F32),
        ],
        scratch_shapes=[
            pltpu.VMEM((SSD_HIST * tb + q, D_XBC), F32),
            pltpu.VMEM((q, D_XBC), F32),
            pltpu.VMEM((q, D_SSM), F32),
            pltpu.VMEM((D_SSM, q), BF16),
            pltpu.VMEM((q, D_SSM), F32),
            pltpu.VMEM((tb, LANES, STATE), F32),
            pltpu.VMEM((3, q, LANES), F32),
        ],
        compiler_params=pltpu.CompilerParams(
            dimension_semantics=("parallel", "arbitrary"), vmem_limit_bytes=VMEM_LIMIT),
        name="ssd_sample",
    )(proj, proj, dtraw, h0, cb0, cw, cbias, dtb, alog, dexp, nw, expand, *aliased)


def _layernorm_silu_store(vc_tiles, lnw_ref, lnb_ref, y_ref):
    total = vc_tiles[0]
    for v in vc_tiles[1:]:
        total = total + v
    mu = total.sum(axis=-1, keepdims=True) * (1.0 / D_CONV)
    sq = None
    for v in vc_tiles:
        d = v - mu
        sq = d * d if sq is None else sq + d * d
    rstd = lax.rsqrt(sq.sum(axis=-1, keepdims=True) * (1.0 / D_CONV) + EPS)
    for t, v in enumerate(vc_tiles):
        cols = slice(t * LANES, (t + 1) * LANES)
        o = ((v - mu) * rstd) * lnw_ref[:, cols] + lnb_ref[:, cols]
        y_ref[:, cols] = _silu(o).astype(y_ref.dtype)


def _conf_prompt_kernel(a_ref, b_ref, buf_ref, w_ref, bias_ref, lnw_ref, lnb_ref, y_ref, bufout_ref,
                        ext_scr, vc_scr, *, n_steps):
    r = CONF_ROWS
    c = pl.program_id(1)
    acc_rows = SSD_CHUNK

    @pl.when(c == 0)
    def _init():
        for t in range(CONV_TILES):
            slab, j, cols = _tile(t)
            ext_scr[slab, _rows(j, 0, CONF_PAD), :] = buf_ref[:, cols]

    for t in range(CONV_TILES):
        slab, j, cols = _tile(t)
        ext_scr[slab, _rows(j, CONF_HIST, r), :] = a_ref[:, cols] * _sigmoid(b_ref[:, cols])

    def slab_body(slab, carry):
        for j in range(TILE_GROUP):
            for rb in range(r // acc_rows):
                base = rb * acc_rows
                acc = bias_ref[slab, j:j + 1, :] + w_ref[slab, 0, j:j + 1, :] * ext_scr[slab, _rows(j, base, acc_rows), :]
                for k in range(1, CONF_W):
                    acc = acc + w_ref[slab, k, j:j + 1, :] * ext_scr[slab, _rows(j, base + k, acc_rows), :]
                vc_scr[slab, j, base:base + acc_rows, :] = acc
        return carry

    lax.fori_loop(0, CONV_SLABS, slab_body, 0)

    @pl.when(c == n_steps - 1)
    def _state():
        for t in range(CONV_TILES):
            slab, j, cols = _tile(t)
            bufout_ref[:, cols] = ext_scr[slab, _rows(j, r, CONF_PAD), :][:CONF_HIST]

    for t in range(CONV_TILES):
        slab, j, cols = _tile(t)
        ext_scr[slab, _rows(j, 0, CONF_PAD), :] = ext_scr[slab, _rows(j, r, CONF_PAD), :]

    _layernorm_silu_store([vc_scr[t // TILE_GROUP, t % TILE_GROUP] for t in range(CONV_TILES)],
                          lnw_ref, lnb_ref, y_ref)


def _conf_prompt(proj, buf32, layer, w_tiles, bias_tiles, lnw, lnb, *, batch, seq):
    r = CONF_ROWS
    ns = seq // r
    per_layer = lambda b, c: (layer, 0, 0)
    return pl.pallas_call(
        functools.partial(_conf_prompt_kernel, n_steps=ns),
        grid=(batch, ns),
        in_specs=[
            pl.BlockSpec((r, D_CONV), lambda b, c: (b * ns + c, 1)),
            pl.BlockSpec((r, D_CONV), lambda b, c: (b * ns + c, 2)),
            pl.BlockSpec((None, CONF_PAD, D_CONV), lambda b, c: (b, 0, 0)),
            pl.BlockSpec((None, CONV_SLABS, CONF_W, TILE_GROUP, LANES), lambda b, c: (layer, 0, 0, 0, 0)),
            pl.BlockSpec((None, CONV_SLABS, TILE_GROUP, LANES), lambda b, c: (layer, 0, 0, 0)),
            pl.BlockSpec((None, 1, D_CONV), per_layer),
            pl.BlockSpec((None, 1, D_CONV), per_layer),
        ],
        out_specs=[
            pl.BlockSpec((r, D_CONV), lambda b, c: (b * ns + c, 0)),
            pl.BlockSpec((None, CONF_HIST, D_CONV), lambda b, c: (b, 0, 0)),
        ],
        out_shape=[
            jax.ShapeDtypeStruct((proj.shape[0], D_CONV), BF16),
            jax.ShapeDtypeStruct((batch, CONF_HIST, D_CONV), F32),
        ],
        scratch_shapes=[
            pltpu.VMEM((CONV_SLABS, TILE_GROUP * (r + CONF_PAD), LANES), F32),
            pltpu.VMEM((CONV_SLABS, TILE_GROUP, r, LANES), F32),
        ],
        compiler_params=pltpu.CompilerParams(
            dimension_semantics=("parallel", "arbitrary"), vmem_limit_bytes=VMEM_LIMIT),
        name="conf_prompt",
    )(proj, proj, buf32, w_tiles, bias_tiles, lnw, lnb)


N_CONF_SAMPLE_IN = 7


def _conf_sample_kernel(*refs, n_alias):
    a_ref, b_ref, buf_ref, w_ref, bias_ref, lnw_ref, lnb_ref = refs[:N_CONF_SAMPLE_IN]
    y_ref, bufout_ref, ext_scr, vc_scr = refs[N_CONF_SAMPLE_IN + n_alias:]
    tb = TILE_BATCH
    rows = tb * DEC_SEQ
    nh = CONF_HIST * tb

    ext_scr[0:nh, :] = buf_ref[...].reshape(nh, D_CONV)
    ext_scr[nh:nh + rows, :] = a_ref[...] * _sigmoid(b_ref[...])
    bufout_ref[...] = ext_scr[rows:rows + nh, :].reshape(CONF_HIST, tb, D_CONV)
    for t in range(CONV_TILES):
        cols = slice(t * LANES, (t + 1) * LANES)
        acc = bias_ref[:, cols] + w_ref[0:1, cols] * ext_scr[0:rows, cols]
        for k in range(1, CONF_W):
            acc = acc + w_ref[k:k + 1, cols] * ext_scr[k * tb:k * tb + rows, cols]
        vc_scr[:, cols] = acc

    _layernorm_silu_store([vc_scr[:, t * LANES:(t + 1) * LANES] for t in range(CONV_TILES)],
                          lnw_ref, lnb_ref, y_ref)


def _conf_sample(proj, buf, y_full, buf_stack, layer, w, bias, lnw, lnb, *, batch, row0):
    tb = TILE_BATCH
    rows = tb * DEC_SEQ
    rb0 = row0 // rows
    per_layer = lambda i: (layer, 0, 0)
    aliased = [y_full] + ([] if buf_stack is None else [buf_stack])
    aliases = {N_CONF_SAMPLE_IN: 0}
    if buf_stack is not None:
        aliases[N_CONF_SAMPLE_IN + 1] = 1
    return pl.pallas_call(
        functools.partial(_conf_sample_kernel, n_alias=len(aliased)),
        grid=(batch // tb,),
        input_output_aliases=aliases,
        in_specs=[
            pl.BlockSpec((rows, D_CONV), lambda i: (rb0 + i, 1)),
            pl.BlockSpec((rows, D_CONV), lambda i: (rb0 + i, 2)),
            pl.BlockSpec((None, CONF_HIST, tb, D_CONV), lambda i: (layer, 0, i, 0)),
            pl.BlockSpec((None, CONF_W, D_CONV), per_layer),
            pl.BlockSpec((None, 1, D_CONV), per_layer),
            pl.BlockSpec((None, 1, D_CONV), per_layer),
            pl.BlockSpec((None, 1, D_CONV), per_layer),
        ] + [pl.BlockSpec(memory_space=pl.ANY) for _ in aliased],
        out_specs=[
            pl.BlockSpec((rows, D_CONV), lambda i: (rb0 + i, 0)),
            pl.BlockSpec((None, CONF_HIST, tb, D_CONV), lambda i: (layer, 0, i, 0)),
        ],
        out_shape=[
            jax.ShapeDtypeStruct(y_full.shape, BF16),
            jax.ShapeDtypeStruct(buf.shape, F32),
        ],
        scratch_shapes=[
            pltpu.VMEM((CONF_HIST * tb + rows, D_CONV), F32),
            pltpu.VMEM((rows, D_CONV), F32),
        ],
        compiler_params=pltpu.CompilerParams(
            dimension_semantics=("parallel",), vmem_limit_bytes=VMEM_LIMIT),
        name="conf_sample",
    )(proj, proj, buf, w, bias, lnw, lnb, *aliased)


def kernel(x_prompt, x_sample, state_ssm, state_ssd_conv, state_conformer_conv, w_in, ssd_conv_w, ssd_conv_b,
           dt_bias, a_log, d_skip, ssd_norm_w, conf_conv_w, conf_conv_b, conf_norm_w, conf_norm_b, w_out,
           norm_pre_mix, norm_post_mix, norm_pre_mlp, norm_post_mlp, w_up, w_down):
    pb, pl_len, d = x_prompt.shape
    sbatch, s_len, _ = x_sample.shape
    tp = pb * pl_len
    ts = sbatch * s_len
    depth = w_in.shape[0]

    w_in_t = jnp.swapaxes(w_in, 1, 2)
    w_dt = jnp.pad(w_in[..., XBC_END:DT_END], ((0, 0), (0, 0), (0, LANES - HEADS))).astype(BF16)
    w_out16 = w_out.astype(BF16)
    w_down16 = w_down.astype(BF16)

    vec = lambda a: a.reshape(depth, 1, a.shape[-1])
    pad_heads = lambda a: jnp.pad(a, ((0, 0), (0, LANES - HEADS))).reshape(depth, 1, LANES)
    dtb = pad_heads(dt_bias)
    alog = pad_heads(a_log)
    dexp = vec(jnp.repeat(d_skip, HEAD_DIM, axis=-1))
    cbias = vec(ssd_conv_b)
    ssd_nw = vec(ssd_norm_w)
    conf_w_tiles = conf_conv_w.reshape(depth, CONF_W, CONV_SLABS, TILE_GROUP, LANES).transpose(0, 2, 1, 3, 4)
    conf_b_tiles = conf_conv_b.reshape(depth, CONV_SLABS, TILE_GROUP, LANES)
    conf_b = vec(conf_conv_b)
    lnw = vec(conf_norm_w)
    lnb = vec(conf_norm_b)
    n_pre_mix, n_post_mix = vec(norm_pre_mix), vec(norm_post_mix)
    n_pre_mlp, n_post_mlp = vec(norm_pre_mlp), vec(norm_post_mlp)
    expand = (lax.broadcasted_iota(jnp.int32, (LANES, D_SSM), 1) // HEAD_DIM
              == lax.broadcasted_iota(jnp.int32, (LANES, D_SSM), 0)).astype(BF16)
    expand = jnp.tile(expand, (3, 1))

    h0_sample = state_ssm.reshape(depth, sbatch, D_SSM, STATE)
    cb0_sample = jnp.swapaxes(state_ssd_conv, 1, 2)
    conf0_sample = jnp.swapaxes(state_conformer_conv, 1, 2)
    h0_prompt = jnp.zeros((pb, D_SSM, STATE), F32)
    cb0_prompt = jnp.zeros((pb, SUBLANES, D_XBC), F32)
    conf0_prompt = jnp.zeros((pb, CONF_PAD, D_CONV), F32)
    n_tiles = sbatch // TILE_BATCH
    xs_rows = x_sample.reshape(n_tiles, TILE_BATCH, s_len, d).transpose(0, 2, 1, 3).reshape(ts, d)

    x, u, dtraw = _entry_call(x_prompt.reshape(tp, d), xs_rows, n_pre_mix, w_dt)

    tn = 1024
    p_ssm, p_sc, p_cc, s_sc = [], [], [], []
    s_ssm = s_cc = None
    for i in range(depth):
        proj = _in_proj(u, w_in_t, i, tm=1024, tn=tn)
        ssd_y, h_p, sc_p = _ssd_prompt(proj, dtraw, h0_prompt, cb0_prompt, i, ssd_conv_w, cbias, dtb, alog,
                                       dexp, ssd_nw, expand, batch=pb, seq=pl_len)
        ssd_y, s_ssm, sc_s = _ssd_sample(proj, dtraw, h0_sample, cb0_sample, ssd_y, s_ssm, i, ssd_conv_w,
                                         cbias, dtb, alog, dexp, ssd_nw, expand, batch=sbatch, row0=tp)
        conf_y, cc_p = _conf_prompt(proj, conf0_prompt, i, conf_w_tiles, conf_b_tiles, lnw, lnb,
                                    batch=pb, seq=pl_len)
        conf_y, s_cc = _conf_sample(proj, conf0_sample, conf_y, s_cc, i, conf_conv_w, conf_b, lnw, lnb,
                                    batch=sbatch, row0=tp)
        x, u2 = _proj_norm([ssd_y, conf_y], w_out16, i, x, n_post_mix, n_pre_mlp, None, i, name="out_proj")
        hid = _up_proj(u2, w_up, i, tm=1024, tn=tn)
        if i + 1 < depth:
            x, u, dtraw = _proj_norm([hid], w_down16, i, x, n_post_mlp, n_pre_mix, w_dt, i + 1,
                                     name="down_proj")
        else:
            y_p, y_s = _proj_norm([hid], w_down16, i, x, n_post_mlp, None, None, None, split_rows=tp,
                                  name="down_proj")
        p_ssm.append(h_p)
        p_sc.append(sc_p)
        p_cc.append(cc_p)
        s_sc.append(sc_s)

    y_prompt = y_p.reshape(pb, pl_len, d)
    y_sample = y_s.reshape(n_tiles, s_len, TILE_BATCH, d).transpose(0, 2, 1, 3).reshape(sbatch, s_len, d)
    state_shape = lambda b: (depth, b, HEADS, HEAD_DIM, STATE)
    return (y_prompt, y_sample,
            jnp.stack(p_ssm).reshape(state_shape(pb)), jnp.stack(p_sc), jnp.stack(p_cc),
            s_ssm.reshape(state_shape(sbatch)), jnp.swapaxes(jnp.stack(s_sc), 1, 2),
            jnp.swapaxes(s_cc, 1, 2))
```

```python
import functools

import jax
import jax.numpy as jnp
from jax import lax
from jax.experimental import pallas as pl
from jax.experimental.pallas import tpu as pltpu

F32 = jnp.float32
BF16 = jnp.bfloat16
HIGHEST = lax.Precision.HIGHEST

D_MODEL = 2048
DEPTH = 4
D_SSM = 2048
D_CONV = 2048
HEAD_DIM = 64
HEADS = 32
GROUPS = 4
HEADS_PER_GROUP = HEADS // GROUPS
GROUP_W = D_SSM // GROUPS
STATE = 128
SSD_CONV_W = 4
SSD_HIST = SSD_CONV_W - 1
D_XBC = D_SSM + 2 * GROUPS * STATE
CONF_W = 31
CONF_HIST = CONF_W - 1
D_FF = 4 * D_MODEL
EPS = 1e-6

LANES = 128
SUBLANES = 8
TILE_GROUP = 4

N_MAIN = D_SSM + 2 * D_CONV + D_XBC
Z_END = D_SSM
XBC_END = Z_END + D_XBC
DT_END = XBC_END + HEADS

SSD_CHUNK = 128
DEC_SEQ = 8
TILE_BATCH = SSD_CHUNK // DEC_SEQ
STATE_BATCH = 4
CONF_ROWS = 256
CONF_PAD = 32
CONV_TILES = D_CONV // LANES
XBC_TILES = D_XBC // LANES
CONV_SLABS = CONV_TILES // TILE_GROUP
XBC_SLABS = XBC_TILES // TILE_GROUP

VMEM_LIMIT = 56 * 1024 * 1024


def _sigmoid(x):
    return 0.5 + 0.5 * jnp.tanh(0.5 * x)


def _silu(x):
    h = 0.5 * x
    return h + h * jnp.tanh(h)


def _softplus(x):
    return jnp.maximum(x, 0.0) + jnp.log1p(jnp.exp(-jnp.abs(x)))


def _rms_scale(x):
    return x * lax.rsqrt(jnp.mean(x * x, axis=-1, keepdims=True) + EPS)


def _tile(t):
    return t // TILE_GROUP, t % TILE_GROUP, slice(t * LANES, (t + 1) * LANES)


def _rows(j, start, n):
    return pl.ds(TILE_GROUP * start + j, n, stride=TILE_GROUP)


def _entry_kernel(xp_ref, xs_ref, wnext_ref, wdt_ref, xo_ref, u_ref, dt_ref, *, prompt_blocks):
    def emit(x):
        xo_ref[...] = x
        u = (_rms_scale(x) * wnext_ref[...]).astype(BF16)
        u_ref[...] = u
        dt_ref[...] = jnp.dot(u, wdt_ref[...], preferred_element_type=F32)

    i = pl.program_id(0)

    @pl.when(i < prompt_blocks)
    def _prompt():
        emit(xp_ref[...])

    @pl.when(i >= prompt_blocks)
    def _sample():
        emit(xs_ref[...])


def _entry_call(xp, xs, wnext, wdt, *, rows=256):
    tp, d = xp.shape
    ts = xs.shape[0]
    t = tp + ts
    pblocks = tp // rows
    row_spec = pl.BlockSpec((rows, d), lambda i: (i, 0))
    return pl.pallas_call(
        functools.partial(_entry_kernel, prompt_blocks=pblocks),
        grid=(t // rows,),
        in_specs=[
            pl.BlockSpec((rows, d), lambda i: (jnp.minimum(i, pblocks - 1), 0)),
            pl.BlockSpec((rows, d), lambda i: (jnp.maximum(i - pblocks, 0), 0)),
            pl.BlockSpec((None, 1, d), lambda i: (0, 0, 0)),
            pl.BlockSpec((None, d, LANES), lambda i: (0, 0, 0)),
        ],
        out_specs=[row_spec, row_spec, pl.BlockSpec((rows, LANES), lambda i: (i, 0))],
        out_shape=[jax.ShapeDtypeStruct((t, d), F32), jax.ShapeDtypeStruct((t, d), BF16),
                   jax.ShapeDtypeStruct((t, LANES), F32)],
        compiler_params=pltpu.CompilerParams(
            dimension_semantics=("arbitrary",), vmem_limit_bytes=VMEM_LIMIT),
        name="entry_norm",
    )(xp, xs, wnext, wdt)


def _up_proj_kernel(x_ref, w_ref, o_ref, wbf_scr):
    @pl.when(pl.program_id(1) == 0)
    def _round_weights():
        wbf_scr[...] = w_ref[...].astype(BF16)

    h = jnp.maximum(jnp.dot(x_ref[...], wbf_scr[...], preferred_element_type=F32), 0.0)
    o_ref[...] = (h * h).astype(o_ref.dtype)


def _up_proj(x, w, layer, *, tm, tn):
    m, k = x.shape
    n = w.shape[-1]
    return pl.pallas_call(
        _up_proj_kernel,
        grid=(n // tn, m // tm),
        in_specs=[pl.BlockSpec((tm, k), lambda j, i: (i, 0)),
                  pl.BlockSpec((None, k, tn), lambda j, i: (layer, 0, j))],
        out_specs=pl.BlockSpec((tm, tn), lambda j, i: (i, j)),
        out_shape=jax.ShapeDtypeStruct((m, n), BF16),
        scratch_shapes=[pltpu.VMEM((k, tn), BF16)],
        compiler_params=pltpu.CompilerParams(
            dimension_semantics=("parallel", "arbitrary"), vmem_limit_bytes=VMEM_LIMIT),
        name="up_proj",
    )(x, w)


def _in_proj_kernel(x_ref, wt_ref, o_ref, wbf_scr):
    @pl.when(pl.program_id(1) == 0)
    def _round_weights():
        wbf_scr[...] = wt_ref[0].T.astype(BF16)

    o_ref[...] = jnp.dot(x_ref[...], wbf_scr[...], preferred_element_type=F32)


def _in_proj(x, w_t, layer, *, tm, tn):
    m, k = x.shape
    z_blocks, glu_blocks = D_SSM // tn, 2 * D_CONV // tn

    def w_row(j):
        row = jnp.where(j < z_blocks, j * tn,
                        jnp.where(j < z_blocks + glu_blocks, DT_END + (j - z_blocks) * tn,
                                  Z_END + (j - z_blocks - glu_blocks) * tn))
        return pl.multiple_of(row, HEADS)

    return pl.pallas_call(
        _in_proj_kernel,
        grid=(N_MAIN // tn, m // tm),
        in_specs=[pl.BlockSpec((tm, k), lambda j, i: (i, 0)),
                  pl.BlockSpec((pl.Element(1), pl.Element(tn), pl.Element(k)),
                               lambda j, i: (layer, w_row(j), 0))],
        out_specs=pl.BlockSpec((tm, tn), lambda j, i: (i, j)),
        out_shape=jax.ShapeDtypeStruct((m, N_MAIN), F32),
        scratch_shapes=[pltpu.VMEM((k, tn), BF16)],
        compiler_params=pltpu.CompilerParams(
            dimension_semantics=("parallel", "arbitrary"), vmem_limit_bytes=VMEM_LIMIT),
        name="in_proj",
    )(x, w_t)


def _proj_norm_kernel(*refs, part_steps, has_next, has_dt, split_block):
    n_parts = len(part_steps)
    x_refs = refs[:n_parts]
    w_ref, res_ref, wpost_ref = refs[n_parts:n_parts + 3]
    pos = n_parts + 3
    wnext_ref = wdt_ref = u_ref = dt_ref = None
    if has_next:
        wnext_ref = refs[pos]
        pos += 1
    if has_dt:
        wdt_ref = refs[pos]
        pos += 1
    n_xo = 1 if split_block is None else 2
    xo_refs = refs[pos:pos + n_xo]
    pos += n_xo
    if has_next:
        u_ref = refs[pos]
        pos += 1
    if has_dt:
        dt_ref = refs[pos]
        pos += 1
    acc_ref = refs[pos]

    i = pl.program_id(0)
    k = pl.program_id(1)
    n_k = sum(part_steps)

    def finish(y):
        x = res_ref[...] + _rms_scale(y) * wpost_ref[...]
        if split_block is None:
            xo_refs[0][...] = x
        else:
            @pl.when(i < split_block)
            def _head_rows():
                xo_refs[0][...] = x

            @pl.when(i >= split_block)
            def _tail_rows():
                xo_refs[1][...] = x
        if has_next:
            u = (_rms_scale(x) * wnext_ref[...]).astype(BF16)
            u_ref[...] = u
            if has_dt:
                dt_ref[...] = jnp.dot(u, wdt_ref[...], preferred_element_type=F32)

    @pl.when(k == 0)
    def _first_step():
        acc_ref[...] = jnp.dot(x_refs[0][...], w_ref[...], preferred_element_type=F32)

    start = 0
    for xr, steps in zip(x_refs, part_steps):
        lo = max(start, 1)
        hi = start + steps
        if hi > lo:
            @pl.when(jnp.logical_and(k >= lo, k < hi))
            def _later_steps(xr=xr):
                acc_ref[...] += jnp.dot(xr[...], w_ref[...], preferred_element_type=F32)
        start += steps

    @pl.when(k == n_k - 1)
    def _epilogue():
        finish(acc_ref[...])


def _proj_norm(xs, w, layer, res, wpost, wnext, wdt, next_layer, *, split_rows=None, tm=512, tk=2048,
               name="proj_norm"):
    m, d = res.shape
    part_steps = tuple(x.shape[1] // tk for x in xs)
    n_k = sum(part_steps)
    has_next, has_dt = wnext is not None, wdt is not None
    in_specs, start = [], 0
    for steps in part_steps:
        in_specs.append(pl.BlockSpec(
            (tm, tk), functools.partial(lambda i, k, s, n: (i, jnp.clip(k - s, 0, n - 1)), s=start, n=steps)))
        start += steps
    row_spec = pl.BlockSpec((tm, d), lambda i, k: (i, 0))
    in_specs += [pl.BlockSpec((None, tk, d), lambda i, k: (layer, k, 0)), row_spec,
                 pl.BlockSpec((None, 1, d), lambda i, k: (layer, 0, 0))]
    args = [*xs, w, res, wpost]
    if has_next:
        in_specs.append(pl.BlockSpec((None, 1, d), lambda i, k: (next_layer, 0, 0)))
        args.append(wnext)
    if has_dt:
        in_specs.append(pl.BlockSpec((None, d, LANES), lambda i, k: (next_layer, 0, 0)))
        args.append(wdt)
    if split_rows is None:
        split_block = None
        out_specs = [row_spec]
        out_shape = [jax.ShapeDtypeStruct((m, d), F32)]
    else:
        split_block = split_rows // tm
        out_specs = [pl.BlockSpec((tm, d), lambda i, k: (jnp.minimum(i, split_block - 1), 0)),
                     pl.BlockSpec((tm, d), lambda i, k: (jnp.maximum(i - split_block, 0), 0))]
        out_shape = [jax.ShapeDtypeStruct((split_rows, d), F32),
                     jax.ShapeDtypeStruct((m - split_rows, d), F32)]
    if has_next:
        out_specs.append(row_spec)
        out_shape.append(jax.ShapeDtypeStruct((m, d), BF16))
    if has_dt:
        out_specs.append(pl.BlockSpec((tm, LANES), lambda i, k: (i, 0)))
        out_shape.append(jax.ShapeDtypeStruct((m, LANES), F32))
    return pl.pallas_call(
        functools.partial(_proj_norm_kernel, part_steps=part_steps, has_next=has_next, has_dt=has_dt,
                          split_block=split_block),
        grid=(m // tm, n_k),
        in_specs=in_specs,
        out_specs=out_specs,
        out_shape=out_shape,
        scratch_shapes=[pltpu.VMEM((tm, d), F32)],
        compiler_params=pltpu.CompilerParams(
            dimension_semantics=("arbitrary", "arbitrary"), vmem_limit_bytes=VMEM_LIMIT),
        name=name,
    )(*args)


def _nt_dot(a, b):
    return lax.dot_general(a, b, (((1,), (1,)), ((), ())), preferred_element_type=F32)


LOG2E = 1.4426950408889634


def _head_decay_weights(cb, cum2, cum2_t, dt_t, mask, h):
    seg2 = cum2[:, h:h + 1] - cum2_t[h:h + 1, :]
    decay = jnp.exp2(jnp.where(mask, seg2, -jnp.inf)) * dt_t[h:h + 1, :]
    return (cb * decay).astype(BF16)


def _pair_y_diag(cb, cum2, cum2_t, dt_t, mask, h1, pair):
    q = pair.shape[0]
    w_stack = jnp.concatenate([_head_decay_weights(cb, cum2, cum2_t, dt_t, mask, h1),
                               _head_decay_weights(cb, cum2, cum2_t, dt_t, mask, h1 + 1)], axis=0)
    yy = jnp.dot(w_stack, pair.astype(BF16), preferred_element_type=F32)
    lo = lax.broadcasted_iota(jnp.int32, pair.shape, 1) < HEAD_DIM
    return jnp.where(lo, yy[:q], yy[q:])


def _gate_norm_store(y_scr, z_ref, nw_ref, y_ref):
    for g in range(GROUPS):
        cols = slice(g * GROUP_W, (g + 1) * GROUP_W)
        gated = y_scr[:, cols] * _silu(z_ref[:, cols])
        y_ref[:, cols] = (_rms_scale(gated) * nw_ref[:, cols]).astype(y_ref.dtype)


def _expand_heads(v, e3_ref):
    hi = v.astype(BF16)
    r1 = v - hi.astype(F32)
    mid = r1.astype(BF16)
    lo = (r1 - mid.astype(F32)).astype(BF16)
    return jnp.dot(jnp.concatenate([hi, mid, lo], axis=1), e3_ref[...], preferred_element_type=F32)


def _dt_terms(dt_ref, dtb_ref, alog_ref):
    dt = _softplus(dt_ref[...] + dtb_ref[...])
    a = dt * (-jnp.exp(alog_ref[...]))
    return dt, a


def _ssd_prompt_kernel(z_ref, xbc_ref, dt_ref, h0_ref, cb_ref, cw_ref, cbias_ref, dtb_ref, alog_ref,
                       dexp_ref, nw_ref, e_ref, y_ref, hout_ref, cbout_ref,
                       ext_scr, xc_scr, y_scr, ecx_scr, *, n_chunks):
    q = SSD_CHUNK
    half = HEAD_DIM
    c = pl.program_id(1)

    @pl.when(c == 0)
    def _init():
        hout_ref[...] = h0_ref[...]
        for t in range(XBC_TILES):
            slab, j, cols = _tile(t)
            ext_scr[slab, _rows(j, 0, SUBLANES), :] = cb_ref[:, cols]

    dt, a = _dt_terms(dt_ref, dtb_ref, alog_ref)
    row = lax.broadcasted_iota(jnp.int32, (q, q), 0)
    col = lax.broadcasted_iota(jnp.int32, (q, q), 1)
    causal = row >= col
    cum = jnp.dot(causal.astype(F32), a, precision=HIGHEST, preferred_element_type=F32)
    cum_t = cum.T
    dt_t = dt.T
    tot = jnp.broadcast_to(cum_t[:, q - 1:q], (LANES, q))
    wend_t = jnp.exp(tot - cum_t) * dt_t
    cdec = jnp.exp(tot)
    ecx_scr[...] = _expand_heads(jnp.exp(cum), e_ref)
    cum2 = cum * LOG2E
    cum2_t = cum_t * LOG2E

    for t in range(XBC_TILES):
        slab, j, cols = _tile(t)
        ext_scr[slab, _rows(j, SSD_HIST, q), :] = xbc_ref[:, cols]
    for t in range(XBC_TILES):
        slab, j, cols = _tile(t)
        acc = cbias_ref[:, cols] + cw_ref[0:1, cols] * ext_scr[slab, _rows(j, 0, q), :]
        for k in range(1, SSD_CONV_W):
            acc = acc + cw_ref[k:k + 1, cols] * ext_scr[slab, _rows(j, k, q), :]
        xc_scr[:, cols] = _silu(acc)
    for t in range(XBC_TILES):
        slab, j, cols = _tile(t)
        ext_scr[slab, _rows(j, 0, SUBLANES), :] = ext_scr[slab, _rows(j, q, SUBLANES), :]

    for g in range(GROUPS):
        gcols = slice(g * GROUP_W, (g + 1) * GROUP_W)
        bcols = slice(D_SSM + g * STATE, D_SSM + (g + 1) * STATE)
        ccols = slice(D_SSM + (GROUPS + g) * STATE, D_SSM + (GROUPS + g + 1) * STATE)
        bg16 = xc_scr[:, bcols].astype(BF16)
        cg16 = xc_scr[:, ccols].astype(BF16)
        cb = _nt_dot(cg16, bg16)
        hg = hout_ref[gcols, :]
        y_off = _nt_dot(cg16, hg.astype(BF16))
        for k in range(HEADS_PER_GROUP // 2):
            h1 = g * HEADS_PER_GROUP + 2 * k
            h2 = h1 + 1
            pr = h1 * HEAD_DIM
            pcols = slice(pr, pr + LANES)
            pair = xc_scr[:, pcols]
            y_scr[:, pcols] = (_pair_y_diag(cb, cum2, cum2_t, dt_t, causal, h1, pair)
                               + y_off[:, k * LANES:(k + 1) * LANES] * ecx_scr[:, pcols]
                               + dexp_ref[:, pcols] * pair)
            pair_t = pair.T
            xw = jnp.concatenate([pair_t[:half] * wend_t[h1:h1 + 1, :],
                                  pair_t[half:] * wend_t[h2:h2 + 1, :]], axis=0).astype(BF16)
            s_new = jnp.dot(xw, bg16, preferred_element_type=F32)
            hp = hg[k * LANES:(k + 1) * LANES]
            hout_ref[pcols, :] = jnp.concatenate([hp[:half] * cdec[h1:h1 + 1, :],
                                                  hp[half:] * cdec[h2:h2 + 1, :]], axis=0) + s_new

    _gate_norm_store(y_scr, z_ref, nw_ref, y_ref)

    @pl.when(c == n_chunks - 1)
    def _conv_state():
        cbout_ref[...] = xbc_ref[q - SSD_HIST:q, :]


def _ssd_prompt(proj, dtraw, h0, cb8, layer, cw, cbias, dtb, alog, dexp, nw, expand, *, batch, seq):
    q = SSD_CHUNK
    nc = seq // q
    xbc_blk = (N_MAIN - D_XBC) // D_XBC
    rowmap = lambda b, c: (b * nc + c, 0)
    per_layer = lambda b, c: (layer, 0, 0)
    per_batch = lambda b, c: (b, 0, 0)
    return pl.pallas_call(
        functools.partial(_ssd_prompt_kernel, n_chunks=nc),
        grid=(batch, nc),
        in_specs=[
            pl.BlockSpec((q, D_SSM), rowmap),
            pl.BlockSpec((q, D_XBC), lambda b, c: (b * nc + c, xbc_blk)),
            pl.BlockSpec((q, LANES), rowmap),
            pl.BlockSpec((None, D_SSM, STATE), per_batch),
            pl.BlockSpec((None, SUBLANES, D_XBC), per_batch),
            pl.BlockSpec((None, SSD_CONV_W, D_XBC), per_layer),
            pl.BlockSpec((None, 1, D_XBC), per_layer),
            pl.BlockSpec((None, 1, LANES), per_layer),
            pl.BlockSpec((None, 1, LANES), per_layer),
            pl.BlockSpec((None, 1, D_SSM), per_layer),
            pl.BlockSpec((None, 1, D_SSM), per_layer),
            pl.BlockSpec((3 * LANES, D_SSM), lambda b, c: (0, 0)),
        ],
        out_specs=[
            pl.BlockSpec((q, D_SSM), rowmap),
            pl.BlockSpec((None, D_SSM, STATE), per_batch),
            pl.BlockSpec((None, SSD_HIST, D_XBC), per_batch),
        ],
        out_shape=[
            jax.ShapeDtypeStruct((proj.shape[0], D_SSM), BF16),
            jax.ShapeDtypeStruct((batch, D_SSM, STATE), F32),
            jax.ShapeDtypeStruct((batch, SSD_HIST, D_XBC), F32),
        ],
        scratch_shapes=[
            pltpu.VMEM((XBC_SLABS, TILE_GROUP * (q + SUBLANES), LANES), F32),
            pltpu.VMEM((q, D_XBC), F32),
            pltpu.VMEM((q, D_SSM), F32),
            pltpu.VMEM((q, D_SSM), F32),
        ],
        compiler_params=pltpu.CompilerParams(
            dimension_semantics=("parallel", "arbitrary"), vmem_limit_bytes=VMEM_LIMIT),
        name="ssd_prompt",
    )(proj, proj, dtraw, h0, cb8, cw, cbias, dtb, alog, dexp, nw, expand)


N_SSD_SAMPLE_IN = 12


def _ssd_sample_kernel(*refs, n_sub, n_alias):
    (z_ref, xbc_ref, dt_ref, h0_ref, cb_ref, cw_ref, cbias_ref, dtb_ref, alog_ref,
     dexp_ref, nw_ref, e_ref) = refs[:N_SSD_SAMPLE_IN]
    (y_ref, hout_ref, cbout_ref,
     ext_scr, xc_scr, y_scr, xwt_scr, ecx_scr, cd_scr, cum_scr) = refs[N_SSD_SAMPLE_IN + n_alias:]
    q = SSD_CHUNK
    tb, sb = TILE_BATCH, STATE_BATCH
    sub = pl.program_id(1)
    row = lax.broadcasted_iota(jnp.int32, (q, q), 0)
    col = lax.broadcasted_iota(jnp.int32, (q, q), 1)
    same = (row % tb) == (col % tb)
    mask = jnp.logical_and(same, row >= col)

    @pl.when(sub == 0)
    def _token_space():
        nh = SSD_HIST * tb
        ext_scr[0:nh, :] = cb_ref[...].reshape(nh, D_XBC)
        ext_scr[nh:nh + q, :] = xbc_ref[...]
        cbout_ref[...] = xbc_ref[q - nh:q, :].reshape(SSD_HIST, tb, D_XBC)
        for t in range(XBC_TILES):
            cols = slice(t * LANES, (t + 1) * LANES)
            acc = cbias_ref[:, cols] + cw_ref[0:1, cols] * ext_scr[0:q, cols]
            for k in range(1, SSD_CONV_W):
                acc = acc + cw_ref[k:k + 1, cols] * ext_scr[k * tb:k * tb + q, cols]
            xc_scr[:, cols] = _silu(acc)

        dt, a = _dt_terms(dt_ref, dtb_ref, alog_ref)
        cum = jnp.dot(mask.astype(F32), a, precision=HIGHEST, preferred_element_type=F32)
        tot = jnp.dot(same.astype(F32), a, precision=HIGHEST, preferred_element_type=F32)
        cum_t = cum.T
        dt_t = dt.T
        cum_scr[0] = cum * LOG2E
        cum_scr[1] = cum_t * LOG2E
        cum_scr[2] = dt_t
        tot_t = tot.T
        wend_t = jnp.exp(tot_t - cum_t) * dt_t
        cdec_t = jnp.exp(tot_t)
        for b in range(tb):
            cd_scr[b] = jnp.broadcast_to(cdec_t[:, b:b + 1], (LANES, STATE))
        ecx_scr[...] = _expand_heads(jnp.exp(cum), e_ref)

        for p in range(HEADS // 2):
            pr = p * LANES
            pair = xc_scr[:, pr:pr + LANES]
            y_scr[:, pr:pr + LANES] = dexp_ref[:, pr:pr + LANES] * pair
            pair_t = pair.T
            for hh in range(2):
                h = 2 * p + hh
                xwt_scr[h * HEAD_DIM:(h + 1) * HEAD_DIM, :] = (
                    pair_t[hh * HEAD_DIM:(hh + 1) * HEAD_DIM, :] * wend_t[h:h + 1, :]).astype(BF16)

    for g in range(GROUPS):
        @pl.when(sub == g)
        def _group_y_diag(g=g):
            bcols = slice(D_SSM + g * STATE, D_SSM + (g + 1) * STATE)
            ccols = slice(D_SSM + (GROUPS + g) * STATE, D_SSM + (GROUPS + g + 1) * STATE)
            cb = _nt_dot(xc_scr[:, ccols].astype(BF16), xc_scr[:, bcols].astype(BF16))
            cum2, cum2_t, dt_t = cum_scr[0], cum_scr[1], cum_scr[2]
            for k in range(HEADS_PER_GROUP // 2):
                h1 = g * HEADS_PER_GROUP + 2 * k
                pr = h1 * HEAD_DIM
                pair = xc_scr[:, pr:pr + LANES]
                y_scr[:, pr:pr + LANES] = (y_scr[:, pr:pr + LANES]
                                           + _pair_y_diag(cb, cum2, cum2_t, dt_t, mask, h1, pair))

    tile_seq = lax.broadcasted_iota(jnp.int32, (q, STATE), 0) % tb
    for g in range(GROUPS):
        gcols = slice(g * GROUP_W, (g + 1) * GROUP_W)
        bcols = slice(D_SSM + g * STATE, D_SSM + (g + 1) * STATE)
        ccols = slice(D_SSM + (GROUPS + g) * STATE, D_SSM + (GROUPS + g + 1) * STATE)
        cg = xc_scr[:, ccols]
        zero = jnp.zeros_like(cg)
        lhs = jnp.concatenate([jnp.where(tile_seq == sub * sb + b, cg, zero) for b in range(sb)],
                              axis=1).astype(BF16)
        hcat = jnp.concatenate([h0_ref[b, gcols, :].astype(BF16) for b in range(sb)], axis=1)
        y_off = _nt_dot(lhs, hcat)
        y_scr[:, gcols] = y_scr[:, gcols] + y_off * ecx_scr[:, gcols]

        bg = xc_scr[:, bcols]
        bzero = jnp.zeros_like(bg)
        bsel = jnp.concatenate([jnp.where(tile_seq == sub * sb + b, bg, bzero) for b in range(sb)],
                               axis=1).astype(BF16)
        s_new = jnp.dot(xwt_scr[gcols, :], bsel, preferred_element_type=F32)
        for b in range(sb):
            cdb = cd_scr[sub * sb + b]
            for hl in range(HEADS_PER_GROUP):
                h = g * HEADS_PER_GROUP + hl
                hrows = slice(h * HEAD_DIM, (h + 1) * HEAD_DIM)
                hout_ref[b, hrows, :] = (
                    h0_ref[b, hrows, :] * cdb[h:h + 1, :]
                    + s_new[hl * HEAD_DIM:(hl + 1) * HEAD_DIM, b * STATE:(b + 1) * STATE])

    @pl.when(sub == n_sub - 1)
    def _finish():
        _gate_norm_store(y_scr, z_ref, nw_ref, y_ref)


def _ssd_sample(proj, dtraw, h0, cb0, y_full, h_stack, layer, cw, cbias, dtb, alog, dexp, nw, expand,
                *, batch, row0):
    q = SSD_CHUNK
    tb, sb = TILE_BATCH, STATE_BATCH
    n_sub = tb // sb
    assert n_sub == GROUPS
    n_tiles = batch // tb
    rb0 = row0 // q
    xbc_blk = (N_MAIN - D_XBC) // D_XBC
    rowmap = lambda i, s: (rb0 + i, 0)
    per_layer = lambda i, s: (layer, 0, 0)
    aliased = [y_full] + ([] if h_stack is None else [h_stack])
    aliases = {N_SSD_SAMPLE_IN: 0}
    if h_stack is not None:
        aliases[N_SSD_SAMPLE_IN + 1] = 1
    return pl.pallas_call(
        functools.partial(_ssd_sample_kernel, n_sub=n_sub, n_alias=len(aliased)),
        grid=(n_tiles, n_sub),
        input_output_aliases=aliases,
        in_specs=[
            pl.BlockSpec((q, D_SSM), rowmap),
            pl.BlockSpec((q, D_XBC), lambda i, s: (rb0 + i, xbc_blk)),
            pl.BlockSpec((q, LANES), rowmap),
            pl.BlockSpec((None, sb, D_SSM, STATE), lambda i, s: (layer, i * n_sub + s, 0, 0)),
            pl.BlockSpec((None, SSD_HIST, tb, D_XBC), lambda i, s: (layer, 0, i, 0)),
            pl.BlockSpec((None, SSD_CONV_W, D_XBC), per_layer),
            pl.BlockSpec((None, 1, D_XBC), per_layer),
            pl.BlockSpec((None, 1, LANES), per_layer),
            pl.BlockSpec((None, 1, LANES), per_layer),
            pl.BlockSpec((None, 1, D_SSM), per_layer),
            pl.BlockSpec((None, 1, D_SSM), per_layer),
            pl.BlockSpec((3 * LANES, D_SSM), lambda i, s: (0, 0)),
        ] + [pl.BlockSpec(memory_space=pl.ANY) for _ in aliased],
        out_specs=[
            pl.BlockSpec((q, D_SSM), rowmap),
            pl.BlockSpec((None, sb, D_SSM, STATE), lambda i, s: (layer, i * n_sub + s, 0, 0)),
            pl.BlockSpec((SSD_HIST, tb, D_XBC), lambda i, s: (0, i, 0)),
        ],
        out_shape=[
            jax.ShapeDtypeStruct(y_full.shape, BF16),
            jax.ShapeDtypeStruct(h0.shape, F32),
            jax.ShapeDtypeStruct((SSD_HIST, batch, D_XBC), F32),
        ],
        scratch_shapes=[
            pltpu.VMEM((SSD_HIST * tb + q, D_XBC), F32),
            pltpu.VMEM((q, D_XBC), F32),
            pltpu.VMEM((q, D_SSM), F32),
            pltpu.VMEM((D_SSM, q), BF16),
            pltpu.VMEM((q, D_SSM), F32),
            pltpu.VMEM((tb, LANES, STATE), F32),
            pltpu.VMEM((3, q, LANES), F32),
        ],
        compiler_params=pltpu.CompilerParams(
            dimension_semantics=("parallel", "arbitrary"), vmem_limit_bytes=VMEM_LIMIT),
        name="ssd_sample",
    )(proj, proj, dtraw, h0, cb0, cw, cbias, dtb, alog, dexp, nw, expand, *aliased)


def _layernorm_silu_store(vc_tiles, lnw_ref, lnb_ref, y_ref):
    total = vc_tiles[0]
    for v in vc_tiles[1:]:
        total = total + v
    mu = total.sum(axis=-1, keepdims=True) * (1.0 / D_CONV)
    sq = None
    for v in vc_tiles:
        d = v - mu
        sq = d * d if sq is None else sq + d * d
    rstd = lax.rsqrt(sq.sum(axis=-1, keepdims=True) * (1.0 / D_CONV) + EPS)
    for t, v in enumerate(vc_tiles):
        cols = slice(t * LANES, (t + 1) * LANES)
        o = ((v - mu) * rstd) * lnw_ref[:, cols] + lnb_ref[:, cols]
        y_ref[:, cols] = _silu(o).astype(y_ref.dtype)


def _conf_prompt_kernel(a_ref, b_ref, buf_ref, w_ref, bias_ref, lnw_ref, lnb_ref, y_ref, bufout_ref,
                        ext_scr, vc_scr, *, n_steps):
    r = CONF_ROWS
    c = pl.program_id(1)
    acc_rows = SSD_CHUNK

    @pl.when(c == 0)
    def _init():
        for t in range(CONV_TILES):
            slab, j, cols = _tile(t)
            ext_scr[slab, _rows(j, 0, CONF_PAD), :] = buf_ref[:, cols]

    for t in range(CONV_TILES):
        slab, j, cols = _tile(t)
        ext_scr[slab, _rows(j, CONF_HIST, r), :] = a_ref[:, cols] * _sigmoid(b_ref[:, cols])

    def slab_body(slab, carry):
        for j in range(TILE_GROUP):
            for rb in range(r // acc_rows):
                base = rb * acc_rows
                acc = bias_ref[slab, j:j + 1, :] + w_ref[slab, 0, j:j + 1, :] * ext_scr[slab, _rows(j, base, acc_rows), :]
                for k in range(1, CONF_W):
                    acc = acc + w_ref[slab, k, j:j + 1, :] * ext_scr[slab, _rows(j, base + k, acc_rows), :]
                vc_scr[slab, j, base:base + acc_rows, :] = acc
        return carry

    lax.fori_loop(0, CONV_SLABS, slab_body, 0)

    @pl.when(c == n_steps - 1)
    def _state():
        for t in range(CONV_TILES):
            slab, j, cols = _tile(t)
            bufout_ref[:, cols] = ext_scr[slab, _rows(j, r, CONF_PAD), :][:CONF_HIST]

    for t in range(CONV_TILES):
        slab, j, cols = _tile(t)
        ext_scr[slab, _rows(j, 0, CONF_PAD), :] = ext_scr[slab, _rows(j, r, CONF_PAD), :]

    _layernorm_silu_store([vc_scr[t // TILE_GROUP, t % TILE_GROUP] for t in range(CONV_TILES)],
                          lnw_ref, lnb_ref, y_ref)


def _conf_prompt(proj, buf32, layer, w_tiles, bias_tiles, lnw, lnb, *, batch, seq):
    r = CONF_ROWS
    ns = seq // r
    per_layer = lambda b, c: (layer, 0, 0)
    return pl.pallas_call(
        functools.partial(_conf_prompt_kernel, n_steps=ns),
        grid=(batch, ns),
        in_specs=[
            pl.BlockSpec((r, D_CONV), lambda b, c: (b * ns + c, 1)),
            pl.BlockSpec((r, D_CONV), lambda b, c: (b * ns + c, 2)),
            pl.BlockSpec((None, CONF_PAD, D_CONV), lambda b, c: (b, 0, 0)),
            pl.BlockSpec((None, CONV_SLABS, CONF_W, TILE_GROUP, LANES), lambda b, c: (layer, 0, 0, 0, 0)),
            pl.BlockSpec((None, CONV_SLABS, TILE_GROUP, LANES), lambda b, c: (layer, 0, 0, 0)),
            pl.BlockSpec((None, 1, D_CONV), per_layer),
            pl.BlockSpec((None, 1, D_CONV), per_layer),
        ],
        out_specs=[
            pl.BlockSpec((r, D_CONV), lambda b, c: (b * ns + c, 0)),
            pl.BlockSpec((None, CONF_HIST, D_CONV), lambda b, c: (b, 0, 0)),
        ],
        out_shape=[
            jax.ShapeDtypeStruct((proj.shape[0], D_CONV), BF16),
            jax.ShapeDtypeStruct((batch, CONF_HIST, D_CONV), F32),
        ],
        scratch_shapes=[
            pltpu.VMEM((CONV_SLABS, TILE_GROUP * (r + CONF_PAD), LANES), F32),
            pltpu.VMEM((CONV_SLABS, TILE_GROUP, r, LANES), F32),
        ],
        compiler_params=pltpu.CompilerParams(
            dimension_semantics=("parallel", "arbitrary"), vmem_limit_bytes=VMEM_LIMIT),
        name="conf_prompt",
    )(proj, proj, buf32, w_tiles, bias_tiles, lnw, lnb)


N_CONF_SAMPLE_IN = 7


def _conf_sample_kernel(*refs, n_alias):
    a_ref, b_ref, buf_ref, w_ref, bias_ref, lnw_ref, lnb_ref = refs[:N_CONF_SAMPLE_IN]
    y_ref, bufout_ref, ext_scr, vc_scr = refs[N_CONF_SAMPLE_IN + n_alias:]
    tb = TILE_BATCH
    rows = tb * DEC_SEQ
    nh = CONF_HIST * tb

    ext_scr[0:nh, :] = buf_ref[...].reshape(nh, D_CONV)
    ext_scr[nh:nh + rows, :] = a_ref[...] * _sigmoid(b_ref[...])
    bufout_ref[...] = ext_scr[rows:rows + nh, :].reshape(CONF_HIST, tb, D_CONV)
    for t in range(CONV_TILES):
        cols = slice(t * LANES, (t + 1) * LANES)
        acc = bias_ref[:, cols] + w_ref[0:1, cols] * ext_scr[0:rows, cols]
        for k in range(1, CONF_W):
            acc = acc + w_ref[k:k + 1, cols] * ext_scr[k * tb:k * tb + rows, cols]
        vc_scr[:, cols] = acc

    _layernorm_silu_store([vc_scr[:, t * LANES:(t + 1) * LANES] for t in range(CONV_TILES)],
                          lnw_ref, lnb_ref, y_ref)


def _conf_sample(proj, buf, y_full, buf_stack, layer, w, bias, lnw, lnb, *, batch, row0):
    tb = TILE_BATCH
    rows = tb * DEC_SEQ
    rb0 = row0 // rows
    per_layer = lambda i: (layer, 0, 0)
    aliased = [y_full] + ([] if buf_stack is None else [buf_stack])
    aliases = {N_CONF_SAMPLE_IN: 0}
    if buf_stack is not None:
        aliases[N_CONF_SAMPLE_IN + 1] = 1
    return pl.pallas_call(
        functools.partial(_conf_sample_kernel, n_alias=len(aliased)),
        grid=(batch // tb,),
        input_output_aliases=aliases,
        in_specs=[
            pl.BlockSpec((rows, D_CONV), lambda i: (rb0 + i, 1)),
            pl.BlockSpec((rows, D_CONV), lambda i: (rb0 + i, 2)),
            pl.BlockSpec((None, CONF_HIST, tb, D_CONV), lambda i: (layer, 0, i, 0)),
            pl.BlockSpec((None, CONF_W, D_CONV), per_layer),
            pl.BlockSpec((None, 1, D_CONV), per_layer),
            pl.BlockSpec((None, 1, D_CONV), per_layer),
            pl.BlockSpec((None, 1, D_CONV), per_layer),
        ] + [pl.BlockSpec(memory_space=pl.ANY) for _ in aliased],
        out_specs=[
            pl.BlockSpec((rows, D_CONV), lambda i: (rb0 + i, 0)),
            pl.BlockSpec((None, CONF_HIST, tb, D_CONV), lambda i: (layer, 0, i, 0)),
        ],
        out_shape=[
            jax.ShapeDtypeStruct(y_full.shape, BF16),
            jax.ShapeDtypeStruct(buf.shape, F32),
        ],
        scratch_shapes=[
            pltpu.VMEM((CONF_HIST * tb + rows, D_CONV), F32),
            pltpu.VMEM((rows, D_CONV), F32),
        ],
        compiler_params=pltpu.CompilerParams(
            dimension_semantics=("parallel",), vmem_limit_bytes=VMEM_LIMIT),
        name="conf_sample",
    )(proj, proj, buf, w, bias, lnw, lnb, *aliased)


def kernel(x_prompt, x_sample, state_ssm, state_ssd_conv, state_conformer_conv, w_in, ssd_conv_w, ssd_conv_b,
           dt_bias, a_log, d_skip, ssd_norm_w, conf_conv_w, conf_conv_b, conf_norm_w, conf_norm_b, w_out,
           norm_pre_mix, norm_post_mix, norm_pre_mlp, norm_post_mlp, w_up, w_down):
    pb, pl_len, d = x_prompt.shape
    sbatch, s_len, _ = x_sample.shape
    tp = pb * pl_len
    ts = sbatch * s_len
    depth = w_in.shape[0]

    w_in_t = jnp.swapaxes(w_in, 1, 2)
    w_dt = jnp.pad(w_in[..., XBC_END:DT_END], ((0, 0), (0, 0), (0, LANES - HEADS))).astype(BF16)
    w_out16 = w_out.astype(BF16)
    w_down16 = w_down.astype(BF16)

    vec = lambda a: a.reshape(depth, 1, a.shape[-1])
    pad_heads = lambda a: jnp.pad(a, ((0, 0), (0, LANES - HEADS))).reshape(depth, 1, LANES)
    dtb = pad_heads(dt_bias)
    alog = pad_heads(a_log)
    dexp = vec(jnp.repeat(d_skip, HEAD_DIM, axis=-1))
    cbias = vec(ssd_conv_b)
    ssd_nw = vec(ssd_norm_w)
    conf_w_tiles = conf_conv_w.reshape(depth, CONF_W, CONV_SLABS, TILE_GROUP, LANES).transpose(0, 2, 1, 3, 4)
    conf_b_tiles = conf_conv_b.reshape(depth, CONV_SLABS, TILE_GROUP, LANES)
    conf_b = vec(conf_conv_b)
    lnw = vec(conf_norm_w)
    lnb = vec(conf_norm_b)
    n_pre_mix, n_post_mix = vec(norm_pre_mix), vec(norm_post_mix)
    n_pre_mlp, n_post_mlp = vec(norm_pre_mlp), vec(norm_post_mlp)
    expand = (lax.broadcasted_iota(jnp.int32, (LANES, D_SSM), 1) // HEAD_DIM
              == lax.broadcasted_iota(jnp.int32, (LANES, D_SSM), 0)).astype(BF16)
    expand = jnp.tile(expand, (3, 1))

    h0_sample = state_ssm.reshape(depth, sbatch, D_SSM, STATE)
    cb0_sample = jnp.swapaxes(state_ssd_conv, 1, 2)
    conf0_sample = jnp.swapaxes(state_conformer_conv, 1, 2)
    h0_prompt = jnp.zeros((pb, D_SSM, STATE), F32)
    cb0_prompt = jnp.zeros((pb, SUBLANES, D_XBC), F32)
    conf0_prompt = jnp.zeros((pb, CONF_PAD, D_CONV), F32)
    n_tiles = sbatch // TILE_BATCH
    xs_rows = x_sample.reshape(n_tiles, TILE_BATCH, s_len, d).transpose(0, 2, 1, 3).reshape(ts, d)

    x, u, dtraw = _entry_call(x_prompt.reshape(tp, d), xs_rows, n_pre_mix, w_dt)

    tn = 1024
    p_ssm, p_sc, p_cc, s_sc = [], [], [], []
    s_ssm = s_cc = None
    for i in range(depth):
        proj = _in_proj(u, w_in_t, i, tm=1024, tn=tn)
        ssd_y, h_p, sc_p = _ssd_prompt(proj, dtraw, h0_prompt, cb0_prompt, i, ssd_conv_w, cbias, dtb, alog,
                                       dexp, ssd_nw, expand, batch=pb, seq=pl_len)
        ssd_y, s_ssm, sc_s = _ssd_sample(proj, dtraw, h0_sample, cb0_sample, ssd_y, s_ssm, i, ssd_conv_w,
                                         cbias, dtb, alog, dexp, ssd_nw, expand, batch=sbatch, row0=tp)
        conf_y, cc_p = _conf_prompt(proj, conf0_prompt, i, conf_w_tiles, conf_b_tiles, lnw, lnb,
                                    batch=pb, seq=pl_len)
        conf_y, s_cc = _conf_sample(proj, conf0_sample, conf_y, s_cc, i, conf_conv_w, conf_b, lnw, lnb,
                                    batch=sbatch, row0=tp)
        x, u2 = _proj_norm([ssd_y, conf_y], w_out16, i, x, n_post_mix, n_pre_mlp, None, i, name="out_proj")
        hid = _up_proj(u2, w_up, i, tm=1024, tn=tn)
        if i + 1 < depth:
            x, u, dtraw = _proj_norm([hid], w_down16, i, x, n_post_mlp, n_pre_mix, w_dt, i + 1,
                                     name="down_proj")
        else:
            y_p, y_s = _proj_norm([hid], w_down16, i, x, n_post_mlp, None, None, None, split_rows=tp,
                                  name="down_proj")
        p_ssm.append(h_p)
        p_sc.append(sc_p)
        p_cc.append(cc_p)
        s_sc.append(sc_s)

    y_prompt = y_p.reshape(pb, pl_len, d)
    y_sample = y_s.reshape(n_tiles, s_len, TILE_BATCH, d).transpose(0, 2, 1, 3).reshape(sbatch, s_len, d)
    state_shape = lambda b: (depth, b, HEADS, HEAD_DIM, STATE)
    return (y_prompt, y_sample,
            jnp.stack(p_ssm).reshape(state_shape(pb)), jnp.stack(p_sc), jnp.stack(p_cc),
            s_ssm.reshape(state_shape(sbatch)), jnp.swapaxes(jnp.stack(s_sc), 1, 2),
            jnp.swapaxes(s_cc, 1, 2))
```

```python
import functools

import jax
import jax.numpy as jnp
from jax import lax
from jax.experimental import pallas as pl
from jax.experimental.pallas import tpu as pltpu

F32 = jnp.float32
BF16 = jnp.bfloat16
HIGHEST = lax.Precision.HIGHEST

D_MODEL = 2048
DEPTH = 4
D_SSM = 2048
D_CONV = 2048
HEAD_DIM = 64
HEADS = 32
GROUPS = 4
HEADS_PER_GROUP = HEADS // GROUPS
GROUP_W = D_SSM // GROUPS
STATE = 128
SSD_CONV_W = 4
SSD_HIST = SSD_CONV_W - 1
D_XBC = D_SSM + 2 * GROUPS * STATE
CONF_W = 31
CONF_HIST = CONF_W - 1
D_FF = 4 * D_MODEL
EPS = 1e-6

LANES = 128
SUBLANES = 8
TILE_GROUP = 4

N_MAIN = D_SSM + 2 * D_CONV + D_XBC
Z_END = D_SSM
XBC_END = Z_END + D_XBC
DT_END = XBC_END + HEADS

SSD_CHUNK = 128
DEC_SEQ = 8
TILE_BATCH = SSD_CHUNK // DEC_SEQ
STATE_BATCH = 4
CONF_ROWS = 256
CONF_PAD = 32
CONV_TILES = D_CONV // LANES
XBC_TILES = D_XBC // LANES
CONV_SLABS = CONV_TILES // TILE_GROUP
XBC_SLABS = XBC_TILES // TILE_GROUP

VMEM_LIMIT = 56 * 1024 * 1024


def _sigmoid(x):
    return 0.5 + 0.5 * jnp.tanh(0.5 * x)


def _silu(x):
    h = 0.5 * x
    return h + h * jnp.tanh(h)


def _softplus(x):
    return jnp.maximum(x, 0.0) + jnp.log1p(jnp.exp(-jnp.abs(x)))


def _rms_scale(x):
    return x * lax.rsqrt(jnp.mean(x * x, axis=-1, keepdims=True) + EPS)


def _tile(t):
    return t // TILE_GROUP, t % TILE_GROUP, slice(t * LANES, (t + 1) * LANES)


def _rows(j, start, n):
    return pl.ds(TILE_GROUP * start + j, n, stride=TILE_GROUP)


def _entry_kernel(xp_ref, xs_ref, wnext_ref, wdt_ref, xo_ref, u_ref, dt_ref, *, prompt_blocks):
    def emit(x):
        xo_ref[...] = x
        u = (_rms_scale(x) * wnext_ref[...]).astype(BF16)
        u_ref[...] = u
        dt_ref[...] = jnp.dot(u, wdt_ref[...], preferred_element_type=F32)

    i = pl.program_id(0)

    @pl.when(i < prompt_blocks)
    def _prompt():
        emit(xp_ref[...])

    @pl.when(i >= prompt_blocks)
    def _sample():
        emit(xs_ref[...])


def _entry_call(xp, xs, wnext, wdt, *, rows=256):
    tp, d = xp.shape
    ts = xs.shape[0]
    t = tp + ts
    pblocks = tp // rows
    row_spec = pl.BlockSpec((rows, d), lambda i: (i, 0))
    return pl.pallas_call(
        functools.partial(_entry_kernel, prompt_blocks=pblocks),
        grid=(t // rows,),
        in_specs=[
            pl.BlockSpec((rows, d), lambda i: (jnp.minimum(i, pblocks - 1), 0)),
            pl.BlockSpec((rows, d), lambda i: (jnp.maximum(i - pblocks, 0), 0)),
            pl.BlockSpec((None, 1, d), lambda i: (0, 0, 0)),
            pl.BlockSpec((None, d, LANES), lambda i: (0, 0, 0)),
        ],
        out_specs=[row_spec, row_spec, pl.BlockSpec((rows, LANES), lambda i: (i, 0))],
        out_shape=[jax.ShapeDtypeStruct((t, d), F32), jax.ShapeDtypeStruct((t, d), BF16),
                   jax.ShapeDtypeStruct((t, LANES), F32)],
        compiler_params=pltpu.CompilerParams(
            dimension_semantics=("arbitrary",), vmem_limit_bytes=VMEM_LIMIT),
        name="entry_norm",
    )(xp, xs, wnext, wdt)


def _up_proj_kernel(x_ref, w_ref, o_ref, wbf_scr):
    @pl.when(pl.program_id(1) == 0)
    def _round_weights():
        wbf_scr[...] = w_ref[...].astype(BF16)

    h = jnp.maximum(jnp.dot(x_ref[...], wbf_scr[...], preferred_element_type=F32), 0.0)
    o_ref[...] = (h * h).astype(o_ref.dtype)


def _up_proj(x, w, layer, *, tm, tn):
    m, k = x.shape
    n = w.shape[-1]
    return pl.pallas_call(
        _up_proj_kernel,
        grid=(n // tn, m // tm),
        in_specs=[pl.BlockSpec((tm, k), lambda j, i: (i, 0)),
                  pl.BlockSpec((None, k, tn), lambda j, i: (layer, 0, j))],
        out_specs=pl.BlockSpec((tm, tn), lambda j, i: (i, j)),
        out_shape=jax.ShapeDtypeStruct((m, n), BF16),
        scratch_shapes=[pltpu.VMEM((k, tn), BF16)],
        compiler_params=pltpu.CompilerParams(
            dimension_semantics=("parallel", "arbitrary"), vmem_limit_bytes=VMEM_LIMIT),
        name="up_proj",
    )(x, w)


def _in_proj_kernel(x_ref, wt_ref, o_ref, wbf_scr):
    @pl.when(pl.program_id(1) == 0)
    def _round_weights():
        wbf_scr[...] = wt_ref[0].T.astype(BF16)

    o_ref[...] = jnp.dot(x_ref[...], wbf_scr[...], preferred_element_type=F32)


def _in_proj(x, w_t, layer, *, tm, tn):
    m, k = x.shape
    z_blocks, glu_blocks = D_SSM // tn, 2 * D_CONV // tn

    def w_row(j):
        row = jnp.where(j < z_blocks, j * tn,
                        jnp.where(j < z_blocks + glu_blocks, DT_END + (j - z_blocks) * tn,
                                  Z_END + (j - z_blocks - glu_blocks) * tn))
        return pl.multiple_of(row, HEADS)

    return pl.pallas_call(
        _in_proj_kernel,
        grid=(N_MAIN // tn, m // tm),
        in_specs=[pl.BlockSpec((tm, k), lambda j, i: (i, 0)),
                  pl.BlockSpec((pl.Element(1), pl.Element(tn), pl.Element(k)),
                               lambda j, i: (layer, w_row(j), 0))],
        out_specs=pl.BlockSpec((tm, tn), lambda j, i: (i, j)),
        out_shape=jax.ShapeDtypeStruct((m, N_MAIN), F32),
        scratch_shapes=[pltpu.VMEM((k, tn), BF16)],
        compiler_params=pltpu.CompilerParams(
            dimension_semantics=("parallel", "arbitrary"), vmem_limit_bytes=VMEM_LIMIT),
        name="in_proj",
    )(x, w_t)


def _proj_norm_kernel(*refs, part_steps, has_next, has_dt, split_block):
    n_parts = len(part_steps)
    x_refs = refs[:n_parts]
    w_ref, res_ref, wpost_ref = refs[n_parts:n_parts + 3]
    pos = n_parts + 3
    wnext_ref = wdt_ref = u_ref = dt_ref = None
    if has_next:
        wnext_ref = refs[pos]
        pos += 1
    if has_dt:
        wdt_ref = refs[pos]
        pos += 1
    n_xo = 1 if split_block is None else 2
    xo_refs = refs[pos:pos + n_xo]
    pos += n_xo
    if has_next:
        u_ref = refs[pos]
        pos += 1
    if has_dt:
        dt_ref = refs[pos]
        pos += 1
    acc_ref = refs[pos]

    i = pl.program_id(0)
    k = pl.program_id(1)
    n_k = sum(part_steps)

    def finish(y):
        x = res_ref[...] + _rms_scale(y) * wpost_ref[...]
        if split_block is None:
            xo_refs[0][...] = x
        else:
            @pl.when(i < split_block)
            def _head_rows():
                xo_refs[0][...] = x

            @pl.when(i >= split_block)
            def _tail_rows():
                xo_refs[1][...] = x
        if has_next:
            u = (_rms_scale(x) * wnext_ref[...]).astype(BF16)
            u_ref[...] = u
            if has_dt:
                dt_ref[...] = jnp.dot(u, wdt_ref[...], preferred_element_type=F32)

    @pl.when(k == 0)
    def _first_step():
        acc_ref[...] = jnp.dot(x_refs[0][...], w_ref[...], preferred_element_type=F32)

    start = 0
    for xr, steps in zip(x_refs, part_steps):
        lo = max(start, 1)
        hi = start + steps
        if hi > lo:
            @pl.when(jnp.logical_and(k >= lo, k < hi))
            def _later_steps(xr=xr):
                acc_ref[...] += jnp.dot(xr[...], w_ref[...], preferred_element_type=F32)
        start += steps

    @pl.when(k == n_k - 1)
    def _epilogue():
        finish(acc_ref[...])


def _proj_norm(xs, w, layer, res, wpost, wnext, wdt, next_layer, *, split_rows=None, tm=512, tk=2048,
               name="proj_norm"):
    m, d = res.shape
    part_steps = tuple(x.shape[1] // tk for x in xs)
    n_k = sum(part_steps)
    has_next, has_dt = wnext is not None, wdt is not None
    in_specs, start = [], 0
    for steps in part_steps:
        in_specs.append(pl.BlockSpec(
            (tm, tk), functools.partial(lambda i, k, s, n: (i, jnp.clip(k - s, 0, n - 1)), s=start, n=steps)))
        start += steps
    row_spec = pl.BlockSpec((tm, d), lambda i, k: (i, 0))
    out_row_spec = pl.BlockSpec((tm, d), lambda i, k: (i, 0), pipeline_mode=pl.Buffered(1))
    in_specs += [pl.BlockSpec((None, tk, d), lambda i, k: (layer, k, 0)), row_spec,
                 pl.BlockSpec((None, 1, d), lambda i, k: (layer, 0, 0))]
    args = [*xs, w, res, wpost]
    if has_next:
        in_specs.append(pl.BlockSpec((None, 1, d), lambda i, k: (next_layer, 0, 0)))
        args.append(wnext)
    if has_dt:
        in_specs.append(pl.BlockSpec((None, d, LANES), lambda i, k: (next_layer, 0, 0)))
        args.append(wdt)
    if split_rows is None:
        split_block = None
        out_specs = [out_row_spec]
        out_shape = [jax.ShapeDtypeStruct((m, d), F32)]
    else:
        split_block = split_rows // tm
        out_specs = [pl.BlockSpec((tm, d), lambda i, k: (jnp.minimum(i, split_block - 1), 0)),
                     pl.BlockSpec((tm, d), lambda i, k: (jnp.maximum(i - split_block, 0), 0))]
        out_shape = [jax.ShapeDtypeStruct((split_rows, d), F32),
                     jax.ShapeDtypeStruct((m - split_rows, d), F32)]
    if has_next:
        out_specs.append(out_row_spec)
        out_shape.append(jax.ShapeDtypeStruct((m, d), BF16))
    if has_dt:
        out_specs.append(pl.BlockSpec((tm, LANES), lambda i, k: (i, 0)))
        out_shape.append(jax.ShapeDtypeStruct((m, LANES), F32))
    return pl.pallas_call(
        functools.partial(_proj_norm_kernel, part_steps=part_steps, has_next=has_next, has_dt=has_dt,
                          split_block=split_block),
        grid=(m // tm, n_k),
        in_specs=in_specs,
        out_specs=out_specs,
        out_shape=out_shape,
        scratch_shapes=[pltpu.VMEM((tm, d), F32)],
        compiler_params=pltpu.CompilerParams(
            dimension_semantics=("arbitrary", "arbitrary"), vmem_limit_bytes=VMEM_LIMIT),
        name=name,
    )(*args)


def _nt_dot(a, b):
    return lax.dot_general(a, b, (((1,), (1,)), ((), ())), preferred_element_type=F32)


LOG2E = 1.4426950408889634


def _head_decay_weights(cb, cum2, cum2_t, dt_t, mask, h):
    seg2 = cum2[:, h:h + 1] - cum2_t[h:h + 1, :]
    decay = jnp.exp2(jnp.where(mask, seg2, -jnp.inf)) * dt_t[h:h + 1, :]
    return (cb * decay).astype(BF16)


def _pair_y_diag(cb, cum2, cum2_t, dt_t, mask, h1, pair):
    q = pair.shape[0]
    w_stack = jnp.concatenate([_head_decay_weights(cb, cum2, cum2_t, dt_t, mask, h1),
                               _head_decay_weights(cb, cum2, cum2_t, dt_t, mask, h1 + 1)], axis=0)
    yy = jnp.dot(w_stack, pair.astype(BF16), preferred_element_type=F32)
    lo = lax.broadcasted_iota(jnp.int32, pair.shape, 1) < HEAD_DIM
    return jnp.where(lo, yy[:q], yy[q:])


def _gate_norm_store(y_scr, z_ref, nw_ref, y_ref):
    for g in range(GROUPS):
        cols = slice(g * GROUP_W, (g + 1) * GROUP_W)
        gated = y_scr[:, cols] * _silu(z_ref[:, cols])
        y_ref[:, cols] = (_rms_scale(gated) * nw_ref[:, cols]).astype(y_ref.dtype)


def _expand_heads(v, e3_ref):
    hi = v.astype(BF16)
    r1 = v - hi.astype(F32)
    mid = r1.astype(BF16)
    lo = (r1 - mid.astype(F32)).astype(BF16)
    return jnp.dot(jnp.concatenate([hi, mid, lo], axis=1), e3_ref[...], preferred_element_type=F32)


def _dt_terms(dt_ref, dtb_ref, alog_ref):
    dt = _softplus(dt_ref[...] + dtb_ref[...])
    a = dt * (-jnp.exp(alog_ref[...]))
    return dt, a


def _ssd_prompt_kernel(z_ref, xbc_ref, dt_ref, h0_ref, cb_ref, cw_ref, cbias_ref, dtb_ref, alog_ref,
                       dexp_ref, nw_ref, e_ref, y_ref, hout_ref, cbout_ref,
                       ext_scr, xc_scr, y_scr, ecx_scr, *, n_chunks):
    q = SSD_CHUNK
    half = HEAD_DIM
    c = pl.program_id(1)

    @pl.when(c == 0)
    def _init():
        hout_ref[...] = h0_ref[...]
        for t in range(XBC_TILES):
            slab, j, cols = _tile(t)
            ext_scr[slab, _rows(j, 0, SUBLANES), :] = cb_ref[:, cols]

    dt, a = _dt_terms(dt_ref, dtb_ref, alog_ref)
    row = lax.broadcasted_iota(jnp.int32, (q, q), 0)
    col = lax.broadcasted_iota(jnp.int32, (q, q), 1)
    causal = row >= col
    cum = jnp.dot(causal.astype(F32), a, precision=HIGHEST, preferred_element_type=F32)
    cum_t = cum.T
    dt_t = dt.T
    tot = jnp.broadcast_to(cum_t[:, q - 1:q], (LANES, q))
    wend_t = jnp.exp(tot - cum_t) * dt_t
    cdec = jnp.exp(tot)
    ecx_scr[...] = _expand_heads(jnp.exp(cum), e_ref)
    cum2 = cum * LOG2E
    cum2_t = cum_t * LOG2E

    for t in range(XBC_TILES):
        slab, j, cols = _tile(t)
        ext_scr[slab, _rows(j, SSD_HIST, q), :] = xbc_ref[:, cols]
    for t in range(XBC_TILES):
        slab, j, cols = _tile(t)
        acc = cbias_ref[:, cols] + cw_ref[0:1, cols] * ext_scr[slab, _rows(j, 0, q), :]
        for k in range(1, SSD_CONV_W):
            acc = acc + cw_ref[k:k + 1, cols] * ext_scr[slab, _rows(j, k, q), :]
        xc_scr[:, cols] = _silu(acc)
    for t in range(XBC_TILES):
        slab, j, cols = _tile(t)
        ext_scr[slab, _rows(j, 0, SUBLANES), :] = ext_scr[slab, _rows(j, q, SUBLANES), :]

    for g in range(GROUPS):
        gcols = slice(g * GROUP_W, (g + 1) * GROUP_W)
        bcols = slice(D_SSM + g * STATE, D_SSM + (g + 1) * STATE)
        ccols = slice(D_SSM + (GROUPS + g) * STATE, D_SSM + (GROUPS + g + 1) * STATE)
        bg16 = xc_scr[:, bcols].astype(BF16)
        cg16 = xc_scr[:, ccols].astype(BF16)
        cb = _nt_dot(cg16, bg16)
        hg = hout_ref[gcols, :]
        y_off = _nt_dot(cg16, hg.astype(BF16))
        for k in range(HEADS_PER_GROUP // 2):
            h1 = g * HEADS_PER_GROUP + 2 * k
            h2 = h1 + 1
            pr = h1 * HEAD_DIM
            pcols = slice(pr, pr + LANES)
            pair = xc_scr[:, pcols]
            y_scr[:, pcols] = (_pair_y_diag(cb, cum2, cum2_t, dt_t, causal, h1, pair)
                               + y_off[:, k * LANES:(k + 1) * LANES] * ecx_scr[:, pcols]
                               + dexp_ref[:, pcols] * pair)
            pair_t = pair.T
            xw = jnp.concatenate([pair_t[:half] * wend_t[h1:h1 + 1, :],
                                  pair_t[half:] * wend_t[h2:h2 + 1, :]], axis=0).astype(BF16)
            s_new = jnp.dot(xw, bg16, preferred_element_type=F32)
            hp = hg[k * LANES:(k + 1) * LANES]
            hout_ref[pcols, :] = jnp.concatenate([hp[:half] * cdec[h1:h1 + 1, :],
                                                  hp[half:] * cdec[h2:h2 + 1, :]], axis=0) + s_new

    _gate_norm_store(y_scr, z_ref, nw_ref, y_ref)

    @pl.when(c == n_chunks - 1)
    def _conv_state():
        cbout_ref[...] = xbc_ref[q - SSD_HIST:q, :]


def _ssd_prompt(proj, dtraw, h0, cb8, layer, cw, cbias, dtb, alog, dexp, nw, expand, *, batch, seq):
    q = SSD_CHUNK
    nc = seq // q
    xbc_blk = (N_MAIN - D_XBC) // D_XBC
    rowmap = lambda b, c: (b * nc + c, 0)
    per_layer = lambda b, c: (layer, 0, 0)
    per_batch = lambda b, c: (b, 0, 0)
    return pl.pallas_call(
        functools.partial(_ssd_prompt_kernel, n_chunks=nc),
        grid=(batch, nc),
        in_specs=[
            pl.BlockSpec((q, D_SSM), rowmap),
            pl.BlockSpec((q, D_XBC), lambda b, c: (b * nc + c, xbc_blk)),
            pl.BlockSpec((q, LANES), rowmap),
            pl.BlockSpec((None, D_SSM, STATE), per_batch),
            pl.BlockSpec((None, SUBLANES, D_XBC), per_batch),
            pl.BlockSpec((None, SSD_CONV_W, D_XBC), per_layer),
            pl.BlockSpec((None, 1, D_XBC), per_layer),
            pl.BlockSpec((None, 1, LANES), per_layer),
            pl.BlockSpec((None, 1, LANES), per_layer),
            pl.BlockSpec((None, 1, D_SSM), per_layer),
            pl.BlockSpec((None, 1, D_SSM), per_layer),
            pl.BlockSpec((3 * LANES, D_SSM), lambda b, c: (0, 0)),
        ],
        out_specs=[
            pl.BlockSpec((q, D_SSM), rowmap),
            pl.BlockSpec((None, D_SSM, STATE), per_batch),
            pl.BlockSpec((None, SSD_HIST, D_XBC), per_batch),
        ],
        out_shape=[
            jax.ShapeDtypeStruct((proj.shape[0], D_SSM), BF16),
            jax.ShapeDtypeStruct((batch, D_SSM, STATE), F32),
            jax.ShapeDtypeStruct((batch, SSD_HIST, D_XBC), F32),
        ],
        scratch_shapes=[
            pltpu.VMEM((XBC_SLABS, TILE_GROUP * (q + SUBLANES), LANES), F32),
            pltpu.VMEM((q, D_XBC), F32),
            pltpu.VMEM((q, D_SSM), F32),
            pltpu.VMEM((q, D_SSM), F32),
        ],
        compiler_params=pltpu.CompilerParams(
            dimension_semantics=("parallel", "arbitrary"), vmem_limit_bytes=VMEM_LIMIT),
        name="ssd_prompt",
    )(proj, proj, dtraw, h0, cb8, cw, cbias, dtb, alog, dexp, nw, expand)


N_SSD_SAMPLE_IN = 12


def _ssd_sample_kernel(*refs, n_sub, n_alias):
    (z_ref, xbc_ref, dt_ref, h0_ref, cb_ref, cw_ref, cbias_ref, dtb_ref, alog_ref,
     dexp_ref, nw_ref, e_ref) = refs[:N_SSD_SAMPLE_IN]
    (y_ref, hout_ref, cbout_ref,
     ext_scr, xc_scr, y_scr, xwt_scr, ecx_scr, cd_scr, cum_scr) = refs[N_SSD_SAMPLE_IN + n_alias:]
    q = SSD_CHUNK
    tb, sb = TILE_BATCH, STATE_BATCH
    sub = pl.program_id(1)
    row = lax.broadcasted_iota(jnp.int32, (q, q), 0)
    col = lax.broadcasted_iota(jnp.int32, (q, q), 1)
    same = (row % tb) == (col % tb)
    mask = jnp.logical_and(same, row >= col)

    @pl.when(sub == 0)
    def _token_space():
        nh = SSD_HIST * tb
        ext_scr[0:nh, :] = cb_ref[...].reshape(nh, D_XBC)
        ext_scr[nh:nh + q, :] = xbc_ref[...]
        cbout_ref[...] = xbc_ref[q - nh:q, :].reshape(SSD_HIST, tb, D_XBC)
        for t in range(XBC_TILES):
            cols = slice(t * LANES, (t + 1) * LANES)
            acc = cbias_ref[:, cols] + cw_ref[0:1, cols] * ext_scr[0:q, cols]
            for k in range(1, SSD_CONV_W):
                acc = acc + cw_ref[k:k + 1, cols] * ext_scr[k * tb:k * tb + q, cols]
            xc_scr[:, cols] = _silu(acc)

        dt, a = _dt_terms(dt_ref, dtb_ref, alog_ref)
        cum = jnp.dot(mask.astype(F32), a, precision=HIGHEST, preferred_element_type=F32)
        tot = jnp.dot(same.astype(F32), a, precision=HIGHEST, preferred_element_type=F32)
        cum_t = cum.T
        dt_t = dt.T
        cum_scr[0] = cum * LOG2E
        cum_scr[1] = cum_t * LOG2E
        cum_scr[2] = dt_t
        tot_t = tot.T
        wend_t = jnp.exp(tot_t - cum_t) * dt_t
        cdec_t = jnp.exp(tot_t)
        for b in range(tb):
            cd_scr[b] = jnp.broadcast_to(cdec_t[:, b:b + 1], (LANES, STATE))
        ecx_scr[...] = _expand_heads(jnp.exp(cum), e_ref)

        for p in range(HEADS // 2):
            pr = p * LANES
            pair = xc_scr[:, pr:pr + LANES]
            y_scr[:, pr:pr + LANES] = dexp_ref[:, pr:pr + LANES] * pair
            pair_t = pair.T
            for hh in range(2):
                h = 2 * p + hh
                xwt_scr[h * HEAD_DIM:(h + 1) * HEAD_DIM, :] = (
                    pair_t[hh * HEAD_DIM:(hh + 1) * HEAD_DIM, :] * wend_t[h:h + 1, :]).astype(BF16)

    for g in range(GROUPS):
        @pl.when(sub == g)
        def _group_y_diag(g=g):
            bcols = slice(D_SSM + g * STATE, D_SSM + (g + 1) * STATE)
            ccols = slice(D_SSM + (GROUPS + g) * STATE, D_SSM + (GROUPS + g + 1) * STATE)
            cb = _nt_dot(xc_scr[:, ccols].astype(BF16), xc_scr[:, bcols].astype(BF16))
            cum2, cum2_t, dt_t = cum_scr[0], cum_scr[1], cum_scr[2]
            for k in range(HEADS_PER_GROUP // 2):
                h1 = g * HEADS_PER_GROUP + 2 * k
                pr = h1 * HEAD_DIM
                pair = xc_scr[:, pr:pr + LANES]
                y_scr[:, pr:pr + LANES] = (y_scr[:, pr:pr + LANES]
                                           + _pair_y_diag(cb, cum2, cum2_t, dt_t, mask, h1, pair))

    tile_seq = lax.broadcasted_iota(jnp.int32, (q, STATE), 0) % tb
    for g in range(GROUPS):
        gcols = slice(g * GROUP_W, (g + 1) * GROUP_W)
        bcols = slice(D_SSM + g * STATE, D_SSM + (g + 1) * STATE)
        ccols = slice(D_SSM + (GROUPS + g) * STATE, D_SSM + (GROUPS + g + 1) * STATE)
        cg = xc_scr[:, ccols]
        zero = jnp.zeros_like(cg)
        lhs = jnp.concatenate([jnp.where(tile_seq == sub * sb + b, cg, zero) for b in range(sb)],
                              axis=1).astype(BF16)
        hcat = jnp.concatenate([h0_ref[b, gcols, :].astype(BF16) for b in range(sb)], axis=1)
        y_off = _nt_dot(lhs, hcat)
        y_scr[:, gcols] = y_scr[:, gcols] + y_off * ecx_scr[:, gcols]

        bg = xc_scr[:, bcols]
        bzero = jnp.zeros_like(bg)
        bsel = jnp.concatenate([jnp.where(tile_seq == sub * sb + b, bg, bzero) for b in range(sb)],
                               axis=1).astype(BF16)
        s_new = jnp.dot(xwt_scr[gcols, :], bsel, preferred_element_type=F32)
        for b in range(sb):
            cdb = cd_scr[sub * sb + b]
            for hl in range(HEADS_PER_GROUP):
                h = g * HEADS_PER_GROUP + hl
                hrows = slice(h * HEAD_DIM, (h + 1) * HEAD_DIM)
                hout_ref[b, hrows, :] = (
                    h0_ref[b, hrows, :] * cdb[h:h + 1, :]
                    + s_new[hl * HEAD_DIM:(hl + 1) * HEAD_DIM, b * STATE:(b + 1) * STATE])

    @pl.when(sub == n_sub - 1)
    def _finish():
        _gate_norm_store(y_scr, z_ref, nw_ref, y_ref)


def _ssd_sample(proj, dtraw, h0, cb0, y_full, h_stack, layer, cw, cbias, dtb, alog, dexp, nw, expand,
                *, batch, row0):
    q = SSD_CHUNK
    tb, sb = TILE_BATCH, STATE_BATCH
    n_sub = tb // sb
    assert n_sub == GROUPS
    n_tiles = batch // tb
    rb0 = row0 // q
    xbc_blk = (N_MAIN - D_XBC) // D_XBC
    rowmap = lambda i, s: (rb0 + i, 0)
    per_layer = lambda i, s: (layer, 0, 0)
    aliased = [y_full] + ([] if h_stack is None else [h_stack])
    aliases = {N_SSD_SAMPLE_IN: 0}
    if h_stack is not None:
        aliases[N_SSD_SAMPLE_IN + 1] = 1
    return pl.pallas_call(
        functools.partial(_ssd_sample_kernel, n_sub=n_sub, n_alias=len(aliased)),
        grid=(n_tiles, n_sub),
        input_output_aliases=aliases,
        in_specs=[
            pl.BlockSpec((q, D_SSM), rowmap),
            pl.BlockSpec((q, D_XBC), lambda i, s: (rb0 + i, xbc_blk)),
            pl.BlockSpec((q, LANES), rowmap),
            pl.BlockSpec((None, sb, D_SSM, STATE), lambda i, s: (layer, i * n_sub + s, 0, 0)),
            pl.BlockSpec((None, SSD_HIST, tb, D_XBC), lambda i, s: (layer, 0, i, 0)),
            pl.BlockSpec((None, SSD_CONV_W, D_XBC), per_layer),
            pl.BlockSpec((None, 1, D_XBC), per_layer),
            pl.BlockSpec((None, 1, LANES), per_layer),
            pl.BlockSpec((None, 1, LANES), per_layer),
            pl.BlockSpec((None, 1, D_SSM), per_layer),
            pl.BlockSpec((None, 1, D_SSM), per_layer),
            pl.BlockSpec((3 * LANES, D_SSM), lambda i, s: (0, 0)),
        ] + [pl.BlockSpec(memory_space=pl.ANY) for _ in aliased],
        out_specs=[
            pl.BlockSpec((q, D_SSM), rowmap),
            pl.BlockSpec((None, sb, D_SSM, STATE), lambda i, s: (layer, i * n_sub + s, 0, 0)),
            pl.BlockSpec((SSD_HIST, tb, D_XBC), lambda i, s: (0, i, 0)),
        ],
        out_shape=[
            jax.ShapeDtypeStruct(y_full.shape, BF16),
            jax.ShapeDtypeStruct(h0.shape, F32),
            jax.ShapeDtypeStruct((SSD_HIST, batch, D_XBC), F32),
        ],
        scratch_shapes=[
            pltpu.VMEM((SSD_HIST * tb + q, D_XBC), F32),
            pltpu.VMEM((q, D_XBC), F32),
            pltpu.VMEM((q, D_SSM), F32),
            pltpu.VMEM((D_SSM, q), BF16),
            pltpu.VMEM((q, D_SSM), F32),
            pltpu.VMEM((tb, LANES, STATE), F32),
            pltpu.VMEM((3, q, LANES), F32),
        ],
        compiler_params=pltpu.CompilerParams(
            dimension_semantics=("parallel", "arbitrary"), vmem_limit_bytes=VMEM_LIMIT),
        name="ssd_sample",
    )(proj, proj, dtraw, h0, cb0, cw, cbias, dtb, alog, dexp, nw, expand, *aliased)


def _layernorm_silu_store(vc_tiles, lnw_ref, lnb_ref, y_ref):
    total = vc_tiles[0]
    for v in vc_tiles[1:]:
        total = total + v
    mu = total.sum(axis=-1, keepdims=True) * (1.0 / D_CONV)
    sq = None
    for v in vc_tiles:
        d = v - mu
        sq = d * d if sq is None else sq + d * d
    rstd = lax.rsqrt(sq.sum(axis=-1, keepdims=True) * (1.0 / D_CONV) + EPS)
    for t, v in enumerate(vc_tiles):
        cols = slice(t * LANES, (t + 1) * LANES)
        o = ((v - mu) * rstd) * lnw_ref[:, cols] + lnb_ref[:, cols]
        y_ref[:, cols] = _silu(o).astype(y_ref.dtype)


def _conf_prompt_kernel(a_ref, b_ref, buf_ref, w_ref, bias_ref, lnw_ref, lnb_ref, y_ref, bufout_ref,
                        ext_scr, vc_scr, *, n_steps):
    r = CONF_ROWS
    c = pl.program_id(1)
    acc_rows = SSD_CHUNK

    @pl.when(c == 0)
    def _init():
        for t in range(CONV_TILES):
            slab, j, cols = _tile(t)
            ext_scr[slab, _rows(j, 0, CONF_PAD), :] = buf_ref[:, cols]

    for t in range(CONV_TILES):
        slab, j, cols = _tile(t)
        ext_scr[slab, _rows(j, CONF_HIST, r), :] = a_ref[:, cols] * _sigmoid(b_ref[:, cols])

    def slab_body(slab, carry):
        for j in range(TILE_GROUP):
            for rb in range(r // acc_rows):
                base = rb * acc_rows
                acc = bias_ref[slab, j:j + 1, :] + w_ref[slab, 0, j:j + 1, :] * ext_scr[slab, _rows(j, base, acc_rows), :]
                for k in range(1, CONF_W):
                    acc = acc + w_ref[slab, k, j:j + 1, :] * ext_scr[slab, _rows(j, base + k, acc_rows), :]
                vc_scr[slab, j, base:base + acc_rows, :] = acc
        return carry

    lax.fori_loop(0, CONV_SLABS, slab_body, 0)

    @pl.when(c == n_steps - 1)
    def _state():
        for t in range(CONV_TILES):
            slab, j, cols = _tile(t)
            bufout_ref[:, cols] = ext_scr[slab, _rows(j, r, CONF_PAD), :][:CONF_HIST]

    for t in range(CONV_TILES):
        slab, j, cols = _tile(t)
        ext_scr[slab, _rows(j, 0, CONF_PAD), :] = ext_scr[slab, _rows(j, r, CONF_PAD), :]

    _layernorm_silu_store([vc_scr[t // TILE_GROUP, t % TILE_GROUP] for t in range(CONV_TILES)],
                          lnw_ref, lnb_ref, y_ref)


def _conf_prompt(proj, buf32, layer, w_tiles, bias_tiles, lnw, lnb, *, batch, seq):
    r = CONF_ROWS
    ns = seq // r
    per_layer = lambda b, c: (layer, 0, 0)
    return pl.pallas_call(
        functools.partial(_conf_prompt_kernel, n_steps=ns),
        grid=(batch, ns),
        in_specs=[
            pl.BlockSpec((r, D_CONV), lambda b, c: (b * ns + c, 1)),
            pl.BlockSpec((r, D_CONV), lambda b, c: (b * ns + c, 2)),
            pl.BlockSpec((None, CONF_PAD, D_CONV), lambda b, c: (b, 0, 0)),
            pl.BlockSpec((None, CONV_SLABS, CONF_W, TILE_GROUP, LANES), lambda b, c: (layer, 0, 0, 0, 0)),
            pl.BlockSpec((None, CONV_SLABS, TILE_GROUP, LANES), lambda b, c: (layer, 0, 0, 0)),
            pl.BlockSpec((None, 1, D_CONV), per_layer),
            pl.BlockSpec((None, 1, D_CONV), per_layer),
        ],
        out_specs=[
            pl.BlockSpec((r, D_CONV), lambda b, c: (b * ns + c, 0)),
            pl.BlockSpec((None, CONF_HIST, D_CONV), lambda b, c: (b, 0, 0)),
        ],
        out_shape=[
            jax.ShapeDtypeStruct((proj.shape[0], D_CONV), BF16),
            jax.ShapeDtypeStruct((batch, CONF_HIST, D_CONV), F32),
        ],
        scratch_shapes=[
            pltpu.VMEM((CONV_SLABS, TILE_GROUP * (r + CONF_PAD), LANES), F32),
            pltpu.VMEM((CONV_SLABS, TILE_GROUP, r, LANES), F32),
        ],
        compiler_params=pltpu.CompilerParams(
            dimension_semantics=("parallel", "arbitrary"), vmem_limit_bytes=VMEM_LIMIT),
        name="conf_prompt",
    )(proj, proj, buf32, w_tiles, bias_tiles, lnw, lnb)


N_CONF_SAMPLE_IN = 7


def _conf_sample_kernel(*refs, n_alias):
    a_ref, b_ref, buf_ref, w_ref, bias_ref, lnw_ref, lnb_ref = refs[:N_CONF_SAMPLE_IN]
    y_ref, bufout_ref, ext_scr, vc_scr = refs[N_CONF_SAMPLE_IN + n_alias:]
    tb = TILE_BATCH
    rows = tb * DEC_SEQ
    nh = CONF_HIST * tb

    ext_scr[0:nh, :] = buf_ref[...].reshape(nh, D_CONV)
    ext_scr[nh:nh + rows, :] = a_ref[...] * _sigmoid(b_ref[...])
    bufout_ref[...] = ext_scr[rows:rows + nh, :].reshape(CONF_HIST, tb, D_CONV)
    for t in range(CONV_TILES):
        cols = slice(t * LANES, (t + 1) * LANES)
        acc = bias_ref[:, cols] + w_ref[0:1, cols] * ext_scr[0:rows, cols]
        for k in range(1, CONF_W):
            acc = acc + w_ref[k:k + 1, cols] * ext_scr[k * tb:k * tb + rows, cols]
        vc_scr[:, cols] = acc

    _layernorm_silu_store([vc_scr[:, t * LANES:(t + 1) * LANES] for t in range(CONV_TILES)],
                          lnw_ref, lnb_ref, y_ref)


def _conf_sample(proj, buf, y_full, buf_stack, layer, w, bias, lnw, lnb, *, batch, row0):
    tb = TILE_BATCH
    rows = tb * DEC_SEQ
    rb0 = row0 // rows
    per_layer = lambda i: (layer, 0, 0)
    aliased = [y_full] + ([] if buf_stack is None else [buf_stack])
    aliases = {N_CONF_SAMPLE_IN: 0}
    if buf_stack is not None:
        aliases[N_CONF_SAMPLE_IN + 1] = 1
    return pl.pallas_call(
        functools.partial(_conf_sample_kernel, n_alias=len(aliased)),
        grid=(batch // tb,),
        input_output_aliases=aliases,
        in_specs=[
            pl.BlockSpec((rows, D_CONV), lambda i: (rb0 + i, 1)),
            pl.BlockSpec((rows, D_CONV), lambda i: (rb0 + i, 2)),
            pl.BlockSpec((None, CONF_HIST, tb, D_CONV), lambda i: (layer, 0, i, 0)),
            pl.BlockSpec((None, CONF_W, D_CONV), per_layer),
            pl.BlockSpec((None, 1, D_CONV), per_layer),
            pl.BlockSpec((None, 1, D_CONV), per_layer),
            pl.BlockSpec((None, 1, D_CONV), per_layer),
        ] + [pl.BlockSpec(memory_space=pl.ANY) for _ in aliased],
        out_specs=[
            pl.BlockSpec((rows, D_CONV), lambda i: (rb0 + i, 0)),
            pl.BlockSpec((None, CONF_HIST, tb, D_CONV), lambda i: (layer, 0, i, 0)),
        ],
        out_shape=[
            jax.ShapeDtypeStruct(y_full.shape, BF16),
            jax.ShapeDtypeStruct(buf.shape, F32),
        ],
        scratch_shapes=[
            pltpu.VMEM((CONF_HIST * tb + rows, D_CONV), F32),
            pltpu.VMEM((rows, D_CONV), F32),
        ],
        compiler_params=pltpu.CompilerParams(
            dimension_semantics=("parallel",), vmem_limit_bytes=VMEM_LIMIT),
        name="conf_sample",
    )(proj, proj, buf, w, bias, lnw, lnb, *aliased)


def kernel(x_prompt, x_sample, state_ssm, state_ssd_conv, state_conformer_conv, w_in, ssd_conv_w, ssd_conv_b,
           dt_bias, a_log, d_skip, ssd_norm_w, conf_conv_w, conf_conv_b, conf_norm_w, conf_norm_b, w_out,
           norm_pre_mix, norm_post_mix, norm_pre_mlp, norm_post_mlp, w_up, w_down):
    pb, pl_len, d = x_prompt.shape
    sbatch, s_len, _ = x_sample.shape
    tp = pb * pl_len
    ts = sbatch * s_len
    depth = w_in.shape[0]

    w_in_t = jnp.swapaxes(w_in, 1, 2)
    w_dt = jnp.pad(w_in[..., XBC_END:DT_END], ((0, 0), (0, 0), (0, LANES - HEADS))).astype(BF16)
    w_out16 = w_out.astype(BF16)
    w_down16 = w_down.astype(BF16)

    vec = lambda a: a.reshape(depth, 1, a.shape[-1])
    pad_heads = lambda a: jnp.pad(a, ((0, 0), (0, LANES - HEADS))).reshape(depth, 1, LANES)
    dtb = pad_heads(dt_bias)
    alog = pad_heads(a_log)
    dexp = vec(jnp.repeat(d_skip, HEAD_DIM, axis=-1))
    cbias = vec(ssd_conv_b)
    ssd_nw = vec(ssd_norm_w)
    conf_w_tiles = conf_conv_w.reshape(depth, CONF_W, CONV_SLABS, TILE_GROUP, LANES).transpose(0, 2, 1, 3, 4)
    conf_b_tiles = conf_conv_b.reshape(depth, CONV_SLABS, TILE_GROUP, LANES)
    conf_b = vec(conf_conv_b)
    lnw = vec(conf_norm_w)
    lnb = vec(conf_norm_b)
    n_pre_mix, n_post_mix = vec(norm_pre_mix), vec(norm_post_mix)
    n_pre_mlp, n_post_mlp = vec(norm_pre_mlp), vec(norm_post_mlp)
    expand = (lax.broadcasted_iota(jnp.int32, (LANES, D_SSM), 1) // HEAD_DIM
              == lax.broadcasted_iota(jnp.int32, (LANES, D_SSM), 0)).astype(BF16)
    expand = jnp.tile(expand, (3, 1))

    h0_sample = state_ssm.reshape(depth, sbatch, D_SSM, STATE)
    cb0_sample = jnp.swapaxes(state_ssd_conv, 1, 2)
    conf0_sample = jnp.swapaxes(state_conformer_conv, 1, 2)
    h0_prompt = jnp.zeros((pb, D_SSM, STATE), F32)
    cb0_prompt = jnp.zeros((pb, SUBLANES, D_XBC), F32)
    conf0_prompt = jnp.zeros((pb, CONF_PAD, D_CONV), F32)
    n_tiles = sbatch // TILE_BATCH
    xs_rows = x_sample.reshape(n_tiles, TILE_BATCH, s_len, d).transpose(0, 2, 1, 3).reshape(ts, d)

    x, u, dtraw = _entry_call(x_prompt.reshape(tp, d), xs_rows, n_pre_mix, w_dt)

    tn = 1024
    p_ssm, p_sc, p_cc, s_sc = [], [], [], []
    s_ssm = s_cc = None
    for i in range(depth):
        proj = _in_proj(u, w_in_t, i, tm=1024, tn=tn)
        ssd_y, h_p, sc_p = _ssd_prompt(proj, dtraw, h0_prompt, cb0_prompt, i, ssd_conv_w, cbias, dtb, alog,
                                       dexp, ssd_nw, expand, batch=pb, seq=pl_len)
        ssd_y, s_ssm, sc_s = _ssd_sample(proj, dtraw, h0_sample, cb0_sample, ssd_y, s_ssm, i, ssd_conv_w,
                                         cbias, dtb, alog, dexp, ssd_nw, expand, batch=sbatch, row0=tp)
        conf_y, cc_p = _conf_prompt(proj, conf0_prompt, i, conf_w_tiles, conf_b_tiles, lnw, lnb,
                                    batch=pb, seq=pl_len)
        conf_y, s_cc = _conf_sample(proj, conf0_sample, conf_y, s_cc, i, conf_conv_w, conf_b, lnw, lnb,
                                    batch=sbatch, row0=tp)
        x, u2 = _proj_norm([ssd_y, conf_y], w_out16, i, x, n_post_mix, n_pre_mlp, None, i, tm=768, tk=1024,
                           name="out_proj")
        hid = _up_proj(u2, w_up, i, tm=1024, tn=tn)
        if i + 1 < depth:
            x, u, dtraw = _proj_norm([hid], w_down16, i, x, n_post_mlp, n_pre_mix, w_dt, i + 1, tm=768,
                                     name="down_proj")
        else:
            y_p, y_s = _proj_norm([hid], w_down16, i, x, n_post_mlp, None, None, None, split_rows=tp,
                                  name="down_proj")
        p_ssm.append(h_p)
        p_sc.append(sc_p)
        p_cc.append(cc_p)
        s_sc.append(sc_s)

    y_prompt = y_p.reshape(pb, pl_len, d)
    y_sample = y_s.reshape(n_tiles, s_len, TILE_BATCH, d).transpose(0, 2, 1, 3).reshape(sbatch, s_len, d)
    state_shape = lambda b: (depth, b, HEADS, HEAD_DIM, STATE)
    return (y_prompt, y_sample,
            jnp.stack(p_ssm).reshape(state_shape(pb)), jnp.stack(p_sc), jnp.stack(p_cc),
            s_ssm.reshape(state_shape(sbatch)), jnp.swapaxes(jnp.stack(s_sc), 1, 2),
            jnp.swapaxes(s_cc, 1, 2))
```

```python
import functools

import jax
import jax.numpy as jnp
from jax import lax
from jax.experimental import pallas as pl
from jax.experimental.pallas import tpu as pltpu

F32 = jnp.float32
BF16 = jnp.bfloat16
HIGHEST = lax.Precision.HIGHEST

D_MODEL = 2048
DEPTH = 4
D_SSM = 2048
D_CONV = 2048
HEAD_DIM = 64
HEADS = 32
GROUPS = 4
HEADS_PER_GROUP = HEADS // GROUPS
GROUP_W = D_SSM // GROUPS
STATE = 128
SSD_CONV_W = 4
SSD_HIST = SSD_CONV_W - 1
D_XBC = D_SSM + 2 * GROUPS * STATE
CONF_W = 31
CONF_HIST = CONF_W - 1
D_FF = 4 * D_MODEL
EPS = 1e-6

LANES = 128
SUBLANES = 8
TILE_GROUP = 4

N_MAIN = D_SSM + 2 * D_CONV + D_XBC
Z_END = D_SSM
XBC_END = Z_END + D_XBC
DT_END = XBC_END + HEADS

SSD_CHUNK = 128
DEC_SEQ = 8
TILE_BATCH = SSD_CHUNK // DEC_SEQ
STATE_BATCH = 8
CONF_ROWS = 256
CONF_PAD = 32
CONV_TILES = D_CONV // LANES
XBC_TILES = D_XBC // LANES
CONV_SLABS = CONV_TILES // TILE_GROUP
XBC_SLABS = XBC_TILES // TILE_GROUP

VMEM_LIMIT = 56 * 1024 * 1024


def _sigmoid(x):
    return 0.5 + 0.5 * jnp.tanh(0.5 * x)


def _silu(x):
    h = 0.5 * x
    return h + h * jnp.tanh(h)


def _softplus(x):
    return jnp.maximum(x, 0.0) + jnp.log1p(jnp.exp(-jnp.abs(x)))


def _rms_scale(x):
    return x * lax.rsqrt(jnp.mean(x * x, axis=-1, keepdims=True) + EPS)


def _tile(t):
    return t // TILE_GROUP, t % TILE_GROUP, slice(t * LANES, (t + 1) * LANES)


def _rows(j, start, n):
    return pl.ds(TILE_GROUP * start + j, n, stride=TILE_GROUP)


def _entry_kernel(xp_ref, xs_ref, wnext_ref, wdt_ref, xo_ref, u_ref, dt_ref, *, prompt_blocks):
    def emit(x):
        xo_ref[...] = x
        u = (_rms_scale(x) * wnext_ref[...]).astype(BF16)
        u_ref[...] = u
        dt_ref[...] = jnp.dot(u, wdt_ref[...], preferred_element_type=F32)

    i = pl.program_id(0)

    @pl.when(i < prompt_blocks)
    def _prompt():
        emit(xp_ref[...])

    @pl.when(i >= prompt_blocks)
    def _sample():
        emit(xs_ref[...])


def _entry_call(xp, xs, wnext, wdt, *, rows=256):
    tp, d = xp.shape
    ts = xs.shape[0]
    t = tp + ts
    pblocks = tp // rows
    row_spec = pl.BlockSpec((rows, d), lambda i: (i, 0))
    return pl.pallas_call(
        functools.partial(_entry_kernel, prompt_blocks=pblocks),
        grid=(t // rows,),
        in_specs=[
            pl.BlockSpec((rows, d), lambda i: (jnp.minimum(i, pblocks - 1), 0)),
            pl.BlockSpec((rows, d), lambda i: (jnp.maximum(i - pblocks, 0), 0)),
            pl.BlockSpec((None, 1, d), lambda i: (0, 0, 0)),
            pl.BlockSpec((None, d, LANES), lambda i: (0, 0, 0)),
        ],
        out_specs=[row_spec, row_spec, pl.BlockSpec((rows, LANES), lambda i: (i, 0))],
        out_shape=[jax.ShapeDtypeStruct((t, d), F32), jax.ShapeDtypeStruct((t, d), BF16),
                   jax.ShapeDtypeStruct((t, LANES), F32)],
        compiler_params=pltpu.CompilerParams(
            dimension_semantics=("arbitrary",), vmem_limit_bytes=VMEM_LIMIT),
        name="entry_norm",
    )(xp, xs, wnext, wdt)


def _up_proj_kernel(x_ref, w_ref, o_ref, wbf_scr):
    @pl.when(pl.program_id(1) == 0)
    def _round_weights():
        wbf_scr[...] = w_ref[...].astype(BF16)

    h = jnp.maximum(jnp.dot(x_ref[...], wbf_scr[...], preferred_element_type=F32), 0.0)
    o_ref[...] = (h * h).astype(o_ref.dtype)


def _up_proj(x, w, layer, *, tm, tn):
    m, k = x.shape
    n = w.shape[-1]
    return pl.pallas_call(
        _up_proj_kernel,
        grid=(n // tn, m // tm),
        in_specs=[pl.BlockSpec((tm, k), lambda j, i: (i, 0)),
                  pl.BlockSpec((None, k, tn), lambda j, i: (layer, 0, j))],
        out_specs=pl.BlockSpec((tm, tn), lambda j, i: (i, j)),
        out_shape=jax.ShapeDtypeStruct((m, n), BF16),
        scratch_shapes=[pltpu.VMEM((k, tn), BF16)],
        compiler_params=pltpu.CompilerParams(
            dimension_semantics=("parallel", "arbitrary"), vmem_limit_bytes=VMEM_LIMIT),
        name="up_proj",
    )(x, w)


def _in_proj_kernel(x_ref, wt_ref, o_ref, wbf_scr):
    @pl.when(pl.program_id(1) == 0)
    def _round_weights():
        wbf_scr[...] = wt_ref[0].T.astype(BF16)

    o_ref[...] = jnp.dot(x_ref[...], wbf_scr[...], preferred_element_type=F32)


def _in_proj(x, w_t, layer, *, tm, tn):
    m, k = x.shape
    z_blocks, glu_blocks = D_SSM // tn, 2 * D_CONV // tn

    def w_row(j):
        row = jnp.where(j < z_blocks, j * tn,
                        jnp.where(j < z_blocks + glu_blocks, DT_END + (j - z_blocks) * tn,
                                  Z_END + (j - z_blocks - glu_blocks) * tn))
        return pl.multiple_of(row, HEADS)

    return pl.pallas_call(
        _in_proj_kernel,
        grid=(N_MAIN // tn, m // tm),
        in_specs=[pl.BlockSpec((tm, k), lambda j, i: (i, 0)),
                  pl.BlockSpec((pl.Element(1), pl.Element(tn), pl.Element(k)),
                               lambda j, i: (layer, w_row(j), 0))],
        out_specs=pl.BlockSpec((tm, tn), lambda j, i: (i, j)),
        out_shape=jax.ShapeDtypeStruct((m, N_MAIN), F32),
        scratch_shapes=[pltpu.VMEM((k, tn), BF16)],
        compiler_params=pltpu.CompilerParams(
            dimension_semantics=("parallel", "arbitrary"), vmem_limit_bytes=VMEM_LIMIT),
        name="in_proj",
    )(x, w_t)


def _proj_norm_kernel(*refs, part_steps, has_next, has_dt, split_block):
    n_parts = len(part_steps)
    x_refs = refs[:n_parts]
    w_ref, res_ref, wpost_ref = refs[n_parts:n_parts + 3]
    pos = n_parts + 3
    wnext_ref = wdt_ref = u_ref = dt_ref = None
    if has_next:
        wnext_ref = refs[pos]
        pos += 1
    if has_dt:
        wdt_ref = refs[pos]
        pos += 1
    n_xo = 1 if split_block is None else 2
    xo_refs = refs[pos:pos + n_xo]
    pos += n_xo
    if has_next:
        u_ref = refs[pos]
        pos += 1
    if has_dt:
        dt_ref = refs[pos]
        pos += 1
    acc_ref = refs[pos]

    i = pl.program_id(0)
    k = pl.program_id(1)
    n_k = sum(part_steps)

    def finish(y):
        x = res_ref[...] + _rms_scale(y) * wpost_ref[...]
        if split_block is None:
            xo_refs[0][...] = x
        else:
            @pl.when(i < split_block)
            def _head_rows():
                xo_refs[0][...] = x

            @pl.when(i >= split_block)
            def _tail_rows():
                xo_refs[1][...] = x
        if has_next:
            u = (_rms_scale(x) * wnext_ref[...]).astype(BF16)
            u_ref[...] = u
            if has_dt:
                dt_ref[...] = jnp.dot(u, wdt_ref[...], preferred_element_type=F32)

    @pl.when(k == 0)
    def _first_step():
        acc_ref[...] = jnp.dot(x_refs[0][...], w_ref[...], preferred_element_type=F32)

    start = 0
    for xr, steps in zip(x_refs, part_steps):
        lo = max(start, 1)
        hi = start + steps
        if hi > lo:
            @pl.when(jnp.logical_and(k >= lo, k < hi))
            def _later_steps(xr=xr):
                acc_ref[...] += jnp.dot(xr[...], w_ref[...], preferred_element_type=F32)
        start += steps

    @pl.when(k == n_k - 1)
    def _epilogue():
        finish(acc_ref[...])


def _proj_norm(xs, w, layer, res, wpost, wnext, wdt, next_layer, *, split_rows=None, tm=512, tk=2048,
               name="proj_norm"):
    m, d = res.shape
    part_steps = tuple(x.shape[1] // tk for x in xs)
    n_k = sum(part_steps)
    has_next, has_dt = wnext is not None, wdt is not None
    in_specs, start = [], 0
    for steps in part_steps:
        in_specs.append(pl.BlockSpec(
            (tm, tk), functools.partial(lambda i, k, s, n: (i, jnp.clip(k - s, 0, n - 1)), s=start, n=steps)))
        start += steps
    row_spec = pl.BlockSpec((tm, d), lambda i, k: (i, 0))
    in_specs += [pl.BlockSpec((None, tk, d), lambda i, k: (layer, k, 0)), row_spec,
                 pl.BlockSpec((None, 1, d), lambda i, k: (layer, 0, 0))]
    args = [*xs, w, res, wpost]
    if has_next:
        in_specs.append(pl.BlockSpec((None, 1, d), lambda i, k: (next_layer, 0, 0)))
        args.append(wnext)
    if has_dt:
        in_specs.append(pl.BlockSpec((None, d, LANES), lambda i, k: (next_layer, 0, 0)))
        args.append(wdt)
    if split_rows is None:
        split_block = None
        out_specs = [row_spec]
        out_shape = [jax.ShapeDtypeStruct((m, d), F32)]
    else:
        split_block = split_rows // tm
        out_specs = [pl.BlockSpec((tm, d), lambda i, k: (jnp.minimum(i, split_block - 1), 0)),
                     pl.BlockSpec((tm, d), lambda i, k: (jnp.maximum(i - split_block, 0), 0))]
        out_shape = [jax.ShapeDtypeStruct((split_rows, d), F32),
                     jax.ShapeDtypeStruct((m - split_rows, d), F32)]
    if has_next:
        out_specs.append(row_spec)
        out_shape.append(jax.ShapeDtypeStruct((m, d), BF16))
    if has_dt:
        out_specs.append(pl.BlockSpec((tm, LANES), lambda i, k: (i, 0)))
        out_shape.append(jax.ShapeDtypeStruct((m, LANES), F32))
    return pl.pallas_call(
        functools.partial(_proj_norm_kernel, part_steps=part_steps, has_next=has_next, has_dt=has_dt,
                          split_block=split_block),
        grid=(m // tm, n_k),
        in_specs=in_specs,
        out_specs=out_specs,
        out_shape=out_shape,
        scratch_shapes=[pltpu.VMEM((tm, d), F32)],
        compiler_params=pltpu.CompilerParams(
            dimension_semantics=("arbitrary", "arbitrary"), vmem_limit_bytes=VMEM_LIMIT),
        name=name,
    )(*args)


def _nt_dot(a, b):
    return lax.dot_general(a, b, (((1,), (1,)), ((), ())), preferred_element_type=F32)


LOG2E = 1.4426950408889634


def _head_decay_weights(cb, cum2, cum2_t, dt_t, mask, h):
    seg2 = cum2[:, h:h + 1] - cum2_t[h:h + 1, :]
    decay = jnp.exp2(jnp.where(mask, seg2, -jnp.inf)) * dt_t[h:h + 1, :]
    return (cb * decay).astype(BF16)


def _pair_y_diag(cb, cum2, cum2_t, dt_t, mask, h1, pair):
    q = pair.shape[0]
    w_stack = jnp.concatenate([_head_decay_weights(cb, cum2, cum2_t, dt_t, mask, h1),
                               _head_decay_weights(cb, cum2, cum2_t, dt_t, mask, h1 + 1)], axis=0)
    yy = jnp.dot(w_stack, pair.astype(BF16), preferred_element_type=F32)
    lo = lax.broadcasted_iota(jnp.int32, pair.shape, 1) < HEAD_DIM
    return jnp.where(lo, yy[:q], yy[q:])


def _gate_norm_store(y_scr, z_ref, nw_ref, y_ref):
    for g in range(GROUPS):
        cols = slice(g * GROUP_W, (g + 1) * GROUP_W)
        gated = y_scr[:, cols] * _silu(z_ref[:, cols])
        y_ref[:, cols] = (_rms_scale(gated) * nw_ref[:, cols]).astype(y_ref.dtype)


def _expand_heads(v, e3_ref):
    hi = v.astype(BF16)
    r1 = v - hi.astype(F32)
    mid = r1.astype(BF16)
    lo = (r1 - mid.astype(F32)).astype(BF16)
    return jnp.dot(jnp.concatenate([hi, mid, lo], axis=1), e3_ref[...], preferred_element_type=F32)


def _dt_terms(dt_ref, dtb_ref, alog_ref):
    dt = _softplus(dt_ref[...] + dtb_ref[...])
    a = dt * (-jnp.exp(alog_ref[...]))
    return dt, a


def _ssd_prompt_kernel(z_ref, xbc_ref, dt_ref, h0_ref, cb_ref, cw_ref, cbias_ref, dtb_ref, alog_ref,
                       dexp_ref, nw_ref, e_ref, y_ref, hout_ref, cbout_ref,
                       ext_scr, xc_scr, y_scr, ecx_scr, *, n_chunks):
    q = SSD_CHUNK
    half = HEAD_DIM
    c = pl.program_id(1)

    @pl.when(c == 0)
    def _init():
        hout_ref[...] = h0_ref[...]
        for t in range(XBC_TILES):
            slab, j, cols = _tile(t)
            ext_scr[slab, _rows(j, 0, SUBLANES), :] = cb_ref[:, cols]

    dt, a = _dt_terms(dt_ref, dtb_ref, alog_ref)
    row = lax.broadcasted_iota(jnp.int32, (q, q), 0)
    col = lax.broadcasted_iota(jnp.int32, (q, q), 1)
    causal = row >= col
    cum = jnp.dot(causal.astype(F32), a, precision=HIGHEST, preferred_element_type=F32)
    cum_t = cum.T
    dt_t = dt.T
    tot = jnp.broadcast_to(cum_t[:, q - 1:q], (LANES, q))
    wend_t = jnp.exp(tot - cum_t) * dt_t
    cdec = jnp.exp(tot)
    ecx_scr[...] = _expand_heads(jnp.exp(cum), e_ref)
    cum2 = cum * LOG2E
    cum2_t = cum_t * LOG2E

    for t in range(XBC_TILES):
        slab, j, cols = _tile(t)
        ext_scr[slab, _rows(j, SSD_HIST, q), :] = xbc_ref[:, cols]
    for t in range(XBC_TILES):
        slab, j, cols = _tile(t)
        acc = cbias_ref[:, cols] + cw_ref[0:1, cols] * ext_scr[slab, _rows(j, 0, q), :]
        for k in range(1, SSD_CONV_W):
            acc = acc + cw_ref[k:k + 1, cols] * ext_scr[slab, _rows(j, k, q), :]
        xc_scr[:, cols] = _silu(acc)
    for t in range(XBC_TILES):
        slab, j, cols = _tile(t)
        ext_scr[slab, _rows(j, 0, SUBLANES), :] = ext_scr[slab, _rows(j, q, SUBLANES), :]

    for g in range(GROUPS):
        gcols = slice(g * GROUP_W, (g + 1) * GROUP_W)
        bcols = slice(D_SSM + g * STATE, D_SSM + (g + 1) * STATE)
        ccols = slice(D_SSM + (GROUPS + g) * STATE, D_SSM + (GROUPS + g + 1) * STATE)
        bg16 = xc_scr[:, bcols].astype(BF16)
        cg16 = xc_scr[:, ccols].astype(BF16)
        cb = _nt_dot(cg16, bg16)
        hg = hout_ref[gcols, :]
        y_off = _nt_dot(cg16, hg.astype(BF16))
        for k in range(HEADS_PER_GROUP // 2):
            h1 = g * HEADS_PER_GROUP + 2 * k
            h2 = h1 + 1
            pr = h1 * HEAD_DIM
            pcols = slice(pr, pr + LANES)
            pair = xc_scr[:, pcols]
            y_scr[:, pcols] = (_pair_y_diag(cb, cum2, cum2_t, dt_t, causal, h1, pair)
                               + y_off[:, k * LANES:(k + 1) * LANES] * ecx_scr[:, pcols]
                               + dexp_ref[:, pcols] * pair)
            pair_t = pair.T
            xw = jnp.concatenate([pair_t[:half] * wend_t[h1:h1 + 1, :],
                                  pair_t[half:] * wend_t[h2:h2 + 1, :]], axis=0).astype(BF16)
            s_new = jnp.dot(xw, bg16, preferred_element_type=F32)
            hp = hg[k * LANES:(k + 1) * LANES]
            hout_ref[pcols, :] = jnp.concatenate([hp[:half] * cdec[h1:h1 + 1, :],
                                                  hp[half:] * cdec[h2:h2 + 1, :]], axis=0) + s_new

    _gate_norm_store(y_scr, z_ref, nw_ref, y_ref)

    @pl.when(c == n_chunks - 1)
    def _conv_state():
        cbout_ref[...] = xbc_ref[q - SSD_HIST:q, :]


def _ssd_prompt(proj, dtraw, h0, cb8, layer, cw, cbias, dtb, alog, dexp, nw, expand, *, batch, seq):
    q = SSD_CHUNK
    nc = seq // q
    xbc_blk = (N_MAIN - D_XBC) // D_XBC
    rowmap = lambda b, c: (b * nc + c, 0)
    per_layer = lambda b, c: (layer, 0, 0)
    per_batch = lambda b, c: (b, 0, 0)
    return pl.pallas_call(
        functools.partial(_ssd_prompt_kernel, n_chunks=nc),
        grid=(batch, nc),
        in_specs=[
            pl.BlockSpec((q, D_SSM), rowmap),
            pl.BlockSpec((q, D_XBC), lambda b, c: (b * nc + c, xbc_blk)),
            pl.BlockSpec((q, LANES), rowmap),
            pl.BlockSpec((None, D_SSM, STATE), per_batch),
            pl.BlockSpec((None, SUBLANES, D_XBC), per_batch),
            pl.BlockSpec((None, SSD_CONV_W, D_XBC), per_layer),
            pl.BlockSpec((None, 1, D_XBC), per_layer),
            pl.BlockSpec((None, 1, LANES), per_layer),
            pl.BlockSpec((None, 1, LANES), per_layer),
            pl.BlockSpec((None, 1, D_SSM), per_layer),
            pl.BlockSpec((None, 1, D_SSM), per_layer),
            pl.BlockSpec((3 * LANES, D_SSM), lambda b, c: (0, 0)),
        ],
        out_specs=[
            pl.BlockSpec((q, D_SSM), rowmap),
            pl.BlockSpec((None, D_SSM, STATE), per_batch),
            pl.BlockSpec((None, SSD_HIST, D_XBC), per_batch),
        ],
        out_shape=[
            jax.ShapeDtypeStruct((proj.shape[0], D_SSM), BF16),
            jax.ShapeDtypeStruct((batch, D_SSM, STATE), F32),
            jax.ShapeDtypeStruct((batch, SSD_HIST, D_XBC), F32),
        ],
        scratch_shapes=[
            pltpu.VMEM((XBC_SLABS, TILE_GROUP * (q + SUBLANES), LANES), F32),
            pltpu.VMEM((q, D_XBC), F32),
            pltpu.VMEM((q, D_SSM), F32),
            pltpu.VMEM((q, D_SSM), F32),
        ],
        compiler_params=pltpu.CompilerParams(
            dimension_semantics=("parallel", "arbitrary"), vmem_limit_bytes=VMEM_LIMIT),
        name="ssd_prompt",
    )(proj, proj, dtraw, h0, cb8, cw, cbias, dtb, alog, dexp, nw, expand)


N_SSD_SAMPLE_IN = 12


def _ssd_sample_kernel(*refs, n_sub, n_alias):
    (z_ref, xbc_ref, dt_ref, h0_ref, cb_ref, cw_ref, cbias_ref, dtb_ref, alog_ref,
     dexp_ref, nw_ref, e_ref) = refs[:N_SSD_SAMPLE_IN]
    (y_ref, hout_ref, cbout_ref,
     ext_scr, xc_scr, y_scr, xwt_scr, ecx_scr, cd_scr, cum_scr) = refs[N_SSD_SAMPLE_IN + n_alias:]
    q = SSD_CHUNK
    tb, sb = TILE_BATCH, STATE_BATCH
    sub = pl.program_id(1)
    row = lax.broadcasted_iota(jnp.int32, (q, q), 0)
    col = lax.broadcasted_iota(jnp.int32, (q, q), 1)
    same = (row % tb) == (col % tb)
    mask = jnp.logical_and(same, row >= col)

    @pl.when(sub == 0)
    def _token_space():
        nh = SSD_HIST * tb
        ext_scr[0:nh, :] = cb_ref[...].reshape(nh, D_XBC)
        ext_scr[nh:nh + q, :] = xbc_ref[...]
        cbout_ref[...] = xbc_ref[q - nh:q, :].reshape(SSD_HIST, tb, D_XBC)
        for t in range(XBC_TILES):
            cols = slice(t * LANES, (t + 1) * LANES)
            acc = cbias_ref[:, cols] + cw_ref[0:1, cols] * ext_scr[0:q, cols]
            for k in range(1, SSD_CONV_W):
                acc = acc + cw_ref[k:k + 1, cols] * ext_scr[k * tb:k * tb + q, cols]
            xc_scr[:, cols] = _silu(acc)

        dt, a = _dt_terms(dt_ref, dtb_ref, alog_ref)
        cum = jnp.dot(mask.astype(F32), a, precision=HIGHEST, preferred_element_type=F32)
        tot = jnp.dot(same.astype(F32), a, precision=HIGHEST, preferred_element_type=F32)
        cum_t = cum.T
        dt_t = dt.T
        cum_scr[0] = cum * LOG2E
        cum_scr[1] = cum_t * LOG2E
        cum_scr[2] = dt_t
        tot_t = tot.T
        wend_t = jnp.exp(tot_t - cum_t) * dt_t
        cdec_t = jnp.exp(tot_t)
        for b in range(tb):
            cd_scr[b] = jnp.broadcast_to(cdec_t[:, b:b + 1], (LANES, STATE))
        ecx_scr[...] = _expand_heads(jnp.exp(cum), e_ref)

        for p in range(HEADS // 2):
            pr = p * LANES
            pair = xc_scr[:, pr:pr + LANES]
            y_scr[:, pr:pr + LANES] = dexp_ref[:, pr:pr + LANES] * pair
            pair_t = pair.T
            for hh in range(2):
                h = 2 * p + hh
                xwt_scr[h * HEAD_DIM:(h + 1) * HEAD_DIM, :] = (
                    pair_t[hh * HEAD_DIM:(hh + 1) * HEAD_DIM, :] * wend_t[h:h + 1, :]).astype(BF16)

    groups_per_sub = GROUPS // n_sub
    for g in range(GROUPS):
        @pl.when(sub == g // groups_per_sub)
        def _group_y_diag(g=g):
            bcols = slice(D_SSM + g * STATE, D_SSM + (g + 1) * STATE)
            ccols = slice(D_SSM + (GROUPS + g) * STATE, D_SSM + (GROUPS + g + 1) * STATE)
            cb = _nt_dot(xc_scr[:, ccols].astype(BF16), xc_scr[:, bcols].astype(BF16))
            cum2, cum2_t, dt_t = cum_scr[0], cum_scr[1], cum_scr[2]
            for k in range(HEADS_PER_GROUP // 2):
                h1 = g * HEADS_PER_GROUP + 2 * k
                pr = h1 * HEAD_DIM
                pair = xc_scr[:, pr:pr + LANES]
                y_scr[:, pr:pr + LANES] = (y_scr[:, pr:pr + LANES]
                                           + _pair_y_diag(cb, cum2, cum2_t, dt_t, mask, h1, pair))

    tile_seq = lax.broadcasted_iota(jnp.int32, (q, STATE), 0) % tb
    for g in range(GROUPS):
        gcols = slice(g * GROUP_W, (g + 1) * GROUP_W)
        bcols = slice(D_SSM + g * STATE, D_SSM + (g + 1) * STATE)
        ccols = slice(D_SSM + (GROUPS + g) * STATE, D_SSM + (GROUPS + g + 1) * STATE)
        cg = xc_scr[:, ccols]
        zero = jnp.zeros_like(cg)
        lhs = jnp.concatenate([jnp.where(tile_seq == sub * sb + b, cg, zero) for b in range(sb)],
                              axis=1).astype(BF16)
        hcat = jnp.concatenate([h0_ref[b, gcols, :].astype(BF16) for b in range(sb)], axis=1)
        y_off = _nt_dot(lhs, hcat)
        y_scr[:, gcols] = y_scr[:, gcols] + y_off * ecx_scr[:, gcols]

        bg = xc_scr[:, bcols]
        bzero = jnp.zeros_like(bg)
        bsel = jnp.concatenate([jnp.where(tile_seq == sub * sb + b, bg, bzero) for b in range(sb)],
                               axis=1).astype(BF16)
        s_new = jnp.dot(xwt_scr[gcols, :], bsel, preferred_element_type=F32)
        for b in range(sb):
            cdb = cd_scr[sub * sb + b]
            for hl in range(HEADS_PER_GROUP):
                h = g * HEADS_PER_GROUP + hl
                hrows = slice(h * HEAD_DIM, (h + 1) * HEAD_DIM)
                hout_ref[b, hrows, :] = (
                    h0_ref[b, hrows, :] * cdb[h:h + 1, :]
                    + s_new[hl * HEAD_DIM:(hl + 1) * HEAD_DIM, b * STATE:(b + 1) * STATE])

    @pl.when(sub == n_sub - 1)
    def _finish():
        _gate_norm_store(y_scr, z_ref, nw_ref, y_ref)


def _ssd_sample(proj, dtraw, h0, cb0, y_full, h_stack, layer, cw, cbias, dtb, alog, dexp, nw, expand,
                *, batch, row0):
    q = SSD_CHUNK
    tb, sb = TILE_BATCH, STATE_BATCH
    n_sub = tb // sb
    assert GROUPS % n_sub == 0
    n_tiles = batch // tb
    rb0 = row0 // q
    xbc_blk = (N_MAIN - D_XBC) // D_XBC
    rowmap = lambda i, s: (rb0 + i, 0)
    per_layer = lambda i, s: (layer, 0, 0)
    aliased = [y_full] + ([] if h_stack is None else [h_stack])
    aliases = {N_SSD_SAMPLE_IN: 0}
    if h_stack is not None:
        aliases[N_SSD_SAMPLE_IN + 1] = 1
    return pl.pallas_call(
        functools.partial(_ssd_sample_kernel, n_sub=n_sub, n_alias=len(aliased)),
        grid=(n_tiles, n_sub),
        input_output_aliases=aliases,
        in_specs=[
            pl.BlockSpec((q, D_SSM), rowmap),
            pl.BlockSpec((q, D_XBC), lambda i, s: (rb0 + i, xbc_blk)),
            pl.BlockSpec((q, LANES), rowmap),
            pl.BlockSpec((None, sb, D_SSM, STATE), lambda i, s: (layer, i * n_sub + s, 0, 0)),
            pl.BlockSpec((None, SSD_HIST, tb, D_XBC), lambda i, s: (layer, 0, i, 0)),
            pl.BlockSpec((None, SSD_CONV_W, D_XBC), per_layer),
            pl.BlockSpec((None, 1, D_XBC), per_layer),
            pl.BlockSpec((None, 1, LANES), per_layer),
            pl.BlockSpec((None, 1, LANES), per_layer),
            pl.BlockSpec((None, 1, D_SSM), per_layer),
            pl.BlockSpec((None, 1, D_SSM), per_layer),
            pl.BlockSpec((3 * LANES, D_SSM), lambda i, s: (0, 0)),
        ] + [pl.BlockSpec(memory_space=pl.ANY) for _ in aliased],
        out_specs=[
            pl.BlockSpec((q, D_SSM), rowmap),
            pl.BlockSpec((None, sb, D_SSM, STATE), lambda i, s: (layer, i * n_sub + s, 0, 0)),
            pl.BlockSpec((SSD_HIST, tb, D_XBC), lambda i, s: (0, i, 0)),
        ],
        out_shape=[
            jax.ShapeDtypeStruct(y_full.shape, BF16),
            jax.ShapeDtypeStruct(h0.shape, F32),
            jax.ShapeDtypeStruct((SSD_HIST, batch, D_XBC), F32),
        ],
        scratch_shapes=[
            pltpu.VMEM((SSD_HIST * tb + q, D_XBC), F32),
            pltpu.VMEM((q, D_XBC), F32),
            pltpu.VMEM((q, D_SSM), F32),
            pltpu.VMEM((D_SSM, q), BF16),
            pltpu.VMEM((q, D_SSM), F32),
            pltpu.VMEM((tb, LANES, STATE), F32),
            pltpu.VMEM((3, q, LANES), F32),
        ],
        compiler_params=pltpu.CompilerParams(
            dimension_semantics=("parallel", "arbitrary"), vmem_limit_bytes=VMEM_LIMIT),
        name="ssd_sample",
    )(proj, proj, dtraw, h0, cb0, cw, cbias, dtb, alog, dexp, nw, expand, *aliased)


def _layernorm_silu_store(vc_tiles, lnw_ref, lnb_ref, y_ref):
    total = vc_tiles[0]
    for v in vc_tiles[1:]:
        total = total + v
    mu = total.sum(axis=-1, keepdims=True) * (1.0 / D_CONV)
    sq = None
    for v in vc_tiles:
        d = v - mu
        sq = d * d if sq is None else sq + d * d
    rstd = lax.rsqrt(sq.sum(axis=-1, keepdims=True) * (1.0 / D_CONV) + EPS)
    for t, v in enumerate(vc_tiles):
        cols = slice(t * LANES, (t + 1) * LANES)
        o = ((v - mu) * rstd) * lnw_ref[:, cols] + lnb_ref[:, cols]
        y_ref[:, cols] = _silu(o).astype(y_ref.dtype)


def _conf_prompt_kernel(a_ref, b_ref, buf_ref, w_ref, bias_ref, lnw_ref, lnb_ref, y_ref, bufout_ref,
                        ext_scr, vc_scr, *, n_steps):
    r = CONF_ROWS
    c = pl.program_id(1)
    acc_rows = SSD_CHUNK

    @pl.when(c == 0)
    def _init():
        for t in range(CONV_TILES):
            slab, j, cols = _tile(t)
            ext_scr[slab, _rows(j, 0, CONF_PAD), :] = buf_ref[:, cols]

    for t in range(CONV_TILES):
        slab, j, cols = _tile(t)
        ext_scr[slab, _rows(j, CONF_HIST, r), :] = a_ref[:, cols] * _sigmoid(b_ref[:, cols])

    def slab_body(slab, carry):
        for j in range(TILE_GROUP):
            for rb in range(r // acc_rows):
                base = rb * acc_rows
                acc = bias_ref[slab, j:j + 1, :] + w_ref[slab, 0, j:j + 1, :] * ext_scr[slab, _rows(j, base, acc_rows), :]
                for k in range(1, CONF_W):
                    acc = acc + w_ref[slab, k, j:j + 1, :] * ext_scr[slab, _rows(j, base + k, acc_rows), :]
                vc_scr[slab, j, base:base + acc_rows, :] = acc
        return carry

    lax.fori_loop(0, CONV_SLABS, slab_body, 0)

    @pl.when(c == n_steps - 1)
    def _state():
        for t in range(CONV_TILES):
            slab, j, cols = _tile(t)
            bufout_ref[:, cols] = ext_scr[slab, _rows(j, r, CONF_PAD), :][:CONF_HIST]

    for t in range(CONV_TILES):
        slab, j, cols = _tile(t)
        ext_scr[slab, _rows(j, 0, CONF_PAD), :] = ext_scr[slab, _rows(j, r, CONF_PAD), :]

    _layernorm_silu_store([vc_scr[t // TILE_GROUP, t % TILE_GROUP] for t in range(CONV_TILES)],
                          lnw_ref, lnb_ref, y_ref)


def _conf_prompt(proj, buf32, layer, w_tiles, bias_tiles, lnw, lnb, *, batch, seq):
    r = CONF_ROWS
    ns = seq // r
    per_layer = lambda b, c: (layer, 0, 0)
    return pl.pallas_call(
        functools.partial(_conf_prompt_kernel, n_steps=ns),
        grid=(batch, ns),
        in_specs=[
            pl.BlockSpec((r, D_CONV), lambda b, c: (b * ns + c, 1)),
            pl.BlockSpec((r, D_CONV), lambda b, c: (b * ns + c, 2)),
            pl.BlockSpec((None, CONF_PAD, D_CONV), lambda b, c: (b, 0, 0)),
            pl.BlockSpec((None, CONV_SLABS, CONF_W, TILE_GROUP, LANES), lambda b, c: (layer, 0, 0, 0, 0)),
            pl.BlockSpec((None, CONV_SLABS, TILE_GROUP, LANES), lambda b, c: (layer, 0, 0, 0)),
            pl.BlockSpec((None, 1, D_CONV), per_layer),
            pl.BlockSpec((None, 1, D_CONV), per_layer),
        ],
        out_specs=[
            pl.BlockSpec((r, D_CONV), lambda b, c: (b * ns + c, 0)),
            pl.BlockSpec((None, CONF_HIST, D_CONV), lambda b, c: (b, 0, 0)),
        ],
        out_shape=[
            jax.ShapeDtypeStruct((proj.shape[0], D_CONV), BF16),
            jax.ShapeDtypeStruct((batch, CONF_HIST, D_CONV), F32),
        ],
        scratch_shapes=[
            pltpu.VMEM((CONV_SLABS, TILE_GROUP * (r + CONF_PAD), LANES), F32),
            pltpu.VMEM((CONV_SLABS, TILE_GROUP, r, LANES), F32),
        ],
        compiler_params=pltpu.CompilerParams(
            dimension_semantics=("parallel", "arbitrary"), vmem_limit_bytes=VMEM_LIMIT),
        name="conf_prompt",
    )(proj, proj, buf32, w_tiles, bias_tiles, lnw, lnb)


N_CONF_SAMPLE_IN = 7


def _conf_sample_kernel(*refs, n_alias):
    a_ref, b_ref, buf_ref, w_ref, bias_ref, lnw_ref, lnb_ref = refs[:N_CONF_SAMPLE_IN]
    y_ref, bufout_ref, ext_scr, vc_scr = refs[N_CONF_SAMPLE_IN + n_alias:]
    tb = TILE_BATCH
    rows = tb * DEC_SEQ
    nh = CONF_HIST * tb

    ext_scr[0:nh, :] = buf_ref[...].reshape(nh, D_CONV)
    ext_scr[nh:nh + rows, :] = a_ref[...] * _sigmoid(b_ref[...])
    bufout_ref[...] = ext_scr[rows:rows + nh, :].reshape(CONF_HIST, tb, D_CONV)
    for t in range(CONV_TILES):
        cols = slice(t * LANES, (t + 1) * LANES)
        acc = bias_ref[:, cols] + w_ref[0:1, cols] * ext_scr[0:rows, cols]
        for k in range(1, CONF_W):
            acc = acc + w_ref[k:k + 1, cols] * ext_scr[k * tb:k * tb + rows, cols]
        vc_scr[:, cols] = acc

    _layernorm_silu_store([vc_scr[:, t * LANES:(t + 1) * LANES] for t in range(CONV_TILES)],
                          lnw_ref, lnb_ref, y_ref)


def _conf_sample(proj, buf, y_full, buf_stack, layer, w, bias, lnw, lnb, *, batch, row0):
    tb = TILE_BATCH
    rows = tb * DEC_SEQ
    rb0 = row0 // rows
    per_layer = lambda i: (layer, 0, 0)
    aliased = [y_full] + ([] if buf_stack is None else [buf_stack])
    aliases = {N_CONF_SAMPLE_IN: 0}
    if buf_stack is not None:
        aliases[N_CONF_SAMPLE_IN + 1] = 1
    return pl.pallas_call(
        functools.partial(_conf_sample_kernel, n_alias=len(aliased)),
        grid=(batch // tb,),
        input_output_aliases=aliases,
        in_specs=[
            pl.BlockSpec((rows, D_CONV), lambda i: (rb0 + i, 1)),
            pl.BlockSpec((rows, D_CONV), lambda i: (rb0 + i, 2)),
            pl.BlockSpec((None, CONF_HIST, tb, D_CONV), lambda i: (layer, 0, i, 0)),
            pl.BlockSpec((None, CONF_W, D_CONV), per_layer),
            pl.BlockSpec((None, 1, D_CONV), per_layer),
            pl.BlockSpec((None, 1, D_CONV), per_layer),
            pl.BlockSpec((None, 1, D_CONV), per_layer),
        ] + [pl.BlockSpec(memory_space=pl.ANY) for _ in aliased],
        out_specs=[
            pl.BlockSpec((rows, D_CONV), lambda i: (rb0 + i, 0)),
            pl.BlockSpec((None, CONF_HIST, tb, D_CONV), lambda i: (layer, 0, i, 0)),
        ],
        out_shape=[
            jax.ShapeDtypeStruct(y_full.shape, BF16),
            jax.ShapeDtypeStruct(buf.shape, F32),
        ],
        scratch_shapes=[
            pltpu.VMEM((CONF_HIST * tb + rows, D_CONV), F32),
            pltpu.VMEM((rows, D_CONV), F32),
        ],
        compiler_params=pltpu.CompilerParams(
            dimension_semantics=("parallel",), vmem_limit_bytes=VMEM_LIMIT),
        name="conf_sample",
    )(proj, proj, buf, w, bias, lnw, lnb, *aliased)


def kernel(x_prompt, x_sample, state_ssm, state_ssd_conv, state_conformer_conv, w_in, ssd_conv_w, ssd_conv_b,
           dt_bias, a_log, d_skip, ssd_norm_w, conf_conv_w, conf_conv_b, conf_norm_w, conf_norm_b, w_out,
           norm_pre_mix, norm_post_mix, norm_pre_mlp, norm_post_mlp, w_up, w_down):
    pb, pl_len, d = x_prompt.shape
    sbatch, s_len, _ = x_sample.shape
    tp = pb * pl_len
    ts = sbatch * s_len
    depth = w_in.shape[0]

    w_in_t = jnp.swapaxes(w_in, 1, 2)
    w_dt = jnp.pad(w_in[..., XBC_END:DT_END], ((0, 0), (0, 0), (0, LANES - HEADS))).astype(BF16)
    w_out16 = w_out.astype(BF16)
    w_down16 = w_down.astype(BF16)

    vec = lambda a: a.reshape(depth, 1, a.shape[-1])
    pad_heads = lambda a: jnp.pad(a, ((0, 0), (0, LANES - HEADS))).reshape(depth, 1, LANES)
    dtb = pad_heads(dt_bias)
    alog = pad_heads(a_log)
    dexp = vec(jnp.repeat(d_skip, HEAD_DIM, axis=-1))
    cbias = vec(ssd_conv_b)
    ssd_nw = vec(ssd_norm_w)
    conf_w_tiles = conf_conv_w.reshape(depth, CONF_W, CONV_SLABS, TILE_GROUP, LANES).transpose(0, 2, 1, 3, 4)
    conf_b_tiles = conf_conv_b.reshape(depth, CONV_SLABS, TILE_GROUP, LANES)
    conf_b = vec(conf_conv_b)
    lnw = vec(conf_norm_w)
    lnb = vec(conf_norm_b)
    n_pre_mix, n_post_mix = vec(norm_pre_mix), vec(norm_post_mix)
    n_pre_mlp, n_post_mlp = vec(norm_pre_mlp), vec(norm_post_mlp)
    expand = (lax.broadcasted_iota(jnp.int32, (LANES, D_SSM), 1) // HEAD_DIM
              == lax.broadcasted_iota(jnp.int32, (LANES, D_SSM), 0)).astype(BF16)
    expand = jnp.tile(expand, (3, 1))

    h0_sample = state_ssm.reshape(depth, sbatch, D_SSM, STATE)
    cb0_sample = jnp.swapaxes(state_ssd_conv, 1, 2)
    conf0_sample = jnp.swapaxes(state_conformer_conv, 1, 2)
    h0_prompt = jnp.zeros((pb, D_SSM, STATE), F32)
    cb0_prompt = jnp.zeros((pb, SUBLANES, D_XBC), F32)
    conf0_prompt = jnp.zeros((pb, CONF_PAD, D_CONV), F32)
    n_tiles = sbatch // TILE_BATCH
    xs_rows = x_sample.reshape(n_tiles, TILE_BATCH, s_len, d).transpose(0, 2, 1, 3).reshape(ts, d)

    x, u, dtraw = _entry_call(x_prompt.reshape(tp, d), xs_rows, n_pre_mix, w_dt)

    tn = 1024
    p_ssm, p_sc, p_cc, s_sc = [], [], [], []
    s_ssm = s_cc = None
    for i in range(depth):
        proj = _in_proj(u, w_in_t, i, tm=1024, tn=tn)
        ssd_y, h_p, sc_p = _ssd_prompt(proj, dtraw, h0_prompt, cb0_prompt, i, ssd_conv_w, cbias, dtb, alog,
                                       dexp, ssd_nw, expand, batch=pb, seq=pl_len)
        ssd_y, s_ssm, sc_s = _ssd_sample(proj, dtraw, h0_sample, cb0_sample, ssd_y, s_ssm, i, ssd_conv_w,
                                         cbias, dtb, alog, dexp, ssd_nw, expand, batch=sbatch, row0=tp)
        conf_y, cc_p = _conf_prompt(proj, conf0_prompt, i, conf_w_tiles, conf_b_tiles, lnw, lnb,
                                    batch=pb, seq=pl_len)
        conf_y, s_cc = _conf_sample(proj, conf0_sample, conf_y, s_cc, i, conf_conv_w, conf_b, lnw, lnb,
                                    batch=sbatch, row0=tp)
        x, u2 = _proj_norm([ssd_y, conf_y], w_out16, i, x, n_post_mix, n_pre_mlp, None, i, name="out_proj")
        hid = _up_proj(u2, w_up, i, tm=1024, tn=tn)
        if i + 1 < depth:
            x, u, dtraw = _proj_norm([hid], w_down16, i, x, n_post_mlp, n_pre_mix, w_dt, i + 1,
                                     name="down_proj")
        else:
            y_p, y_s = _proj_norm([hid], w_down16, i, x, n_post_mlp, None, None, None, split_rows=tp,
                                  name="down_proj")
        p_ssm.append(h_p)
        p_sc.append(sc_p)
        p_cc.append(cc_p)
        s_sc.append(sc_s)

    y_prompt = y_p.reshape(pb, pl_len, d)
    y_sample = y_s.reshape(n_tiles, s_len, TILE_BATCH, d).transpose(0, 2, 1, 3).reshape(sbatch, s_len, d)
    state_shape = lambda b: (depth, b, HEADS, HEAD_DIM, STATE)
    return (y_prompt, y_sample,
            jnp.stack(p_ssm).reshape(state_shape(pb)), jnp.stack(p_sc), jnp.stack(p_cc),
            s_ssm.reshape(state_shape(sbatch)), jnp.swapaxes(jnp.stack(s_sc), 1, 2),
            jnp.swapaxes(s_cc, 1, 2))
```
